```python
import jax, jax.numpy as jnp
from jax import lax
import numpy as np

D_MODEL = 1024
BATCH = 16
SEQ = 2048
DEPTH = 4

CONV_W = 512
CONV_K = 3
N_HEADS = 8
HEAD_DIM = 64
N_KV = 2
HPG = N_HEADS // N_KV
ATTN_W = N_HEADS * HEAD_DIM
KV_W = N_KV * HEAD_DIM
CMP_L = 32
CMP_D = 16
CMP_HIDDEN = 4 * HEAD_DIM
SEL_L = 64
N_SEL = 8
WINDOW = 512
Q_BLOCK = 128
FORCE_BONUS = 1e4
NEG = -1e30
D_FF = 4 * D_MODEL
ROPE_THETA = 10000.0
EPS = 1e-6
POS_OFFSET_MAX = 1024
IN_SIZES = (CONV_W, CONV_W, CONV_W, ATTN_W, KV_W, KV_W, KV_W, KV_W, KV_W, KV_W, 3 * N_HEADS, 2 * D_MODEL)
IN_W = sum(IN_SIZES)

kernel_name = "hybrid_conv_nsa_adaln_trunk"


def rmsnorm(x, g):
    xf = x.astype(jnp.float32)
    y = xf * lax.rsqrt(jnp.mean(xf * xf, axis=-1, keepdims=True) + EPS)
    return (y * g.astype(jnp.float32)).astype(x.dtype)


def rope_tables(positions):
    half = HEAD_DIM // 2
    inv_freq = ROPE_THETA ** (-jnp.arange(half, dtype=jnp.float32) * 2.0 / HEAD_DIM)
    ang = positions.astype(jnp.float32)[..., None] * inv_freq
    return jnp.cos(ang)[:, :, None, :], jnp.sin(ang)[:, :, None, :]


def apply_rope(x, cos, sin):
    half = HEAD_DIM // 2
    xf = x.astype(jnp.float32)
    x1, x2 = xf[..., :half], xf[..., half:]
    return jnp.concatenate([x1 * cos - x2 * sin, x2 * cos + x1 * sin], axis=-1).astype(x.dtype)


def masked_softmax(s, mask):
    s = jnp.where(mask, s.astype(jnp.float32), NEG)
    p = jax.nn.softmax(s, axis=-1)
    return jnp.where(mask, p, 0.0)


def short_conv_mixer(b_gate, c_gate, u, conv_w):
    v = c_gate * u
    w = conv_w.reshape(CONV_K, 1, CONV_W).astype(v.dtype)
    y = lax.conv_general_dilated(v, w, window_strides=(1,), padding=[(CONV_K - 1, 0)],
                                 dimension_numbers=('NWC', 'WIO', 'NWC'),
                                 feature_group_count=CONV_W)
    return b_gate * y


def compress_blocks(k, w1, w2, pe):
    B, S = k.shape[0], k.shape[1]
    nc = (S - CMP_L) // CMP_D + 1
    idx = np.arange(nc)[:, None] * CMP_D + np.arange(CMP_L)[None, :]
    blocks = jnp.swapaxes(k[:, idx], 2, 3) + pe
    flat = blocks.reshape(B, nc, N_KV, CMP_L * HEAD_DIM)
    return jax.nn.silu(flat @ w1) @ w2


def nsa_mixer(q, kc, vc, ks, vs, kw, vw, gate_logits, ck_w1, ck_w2, ck_pe, cv_w1, cv_w2, cv_pe):
    B, S = q.shape[0], q.shape[1]
    scale = HEAD_DIM ** -0.5
    qg = q.reshape(B, S, N_KV, HPG, HEAD_DIM)
    t = jnp.arange(S)

    kcmp = compress_blocks(kc, ck_w1, ck_w2, ck_pe)
    vcmp = compress_blocks(vc, cv_w1, cv_w2, cv_pe)
    nc = kcmp.shape[1]
    cstart = jnp.arange(nc) * CMP_D
    cmp_mask = (cstart + CMP_L - 1)[None, :] <= t[:, None]
    s_cmp = jnp.einsum('bsghd,bcgd->bsghc', qg, kcmp).astype(jnp.float32) * scale
    p_cmp = masked_softmax(s_cmp, cmp_mask[None, :, None, None, :])
    o_cmp = jnp.einsum('bsghc,bcgd->bsghd', p_cmp.astype(q.dtype), vcmp)

    ns = S // SEL_L
    j = jnp.arange(ns)
    overlap = ((cstart[:, None] < (j[None, :] + 1) * SEL_L) &
               (cstart[:, None] + CMP_L > j[None, :] * SEL_L)).astype(jnp.float32)
    imp = jnp.einsum('bsgc,cn->bsgn', p_cmp.sum(axis=3), overlap)
    blk_t = t // SEL_L
    valid = (j[None, :] <= blk_t[:, None])[:, None, :]
    forced = ((j[None, :] == 0) | (j[None, :] == blk_t[:, None]) | (j[None, :] == blk_t[:, None] - 1))[:, None, :]
    imp = jnp.where(valid, imp + FORCE_BONUS * forced.astype(jnp.float32), -1.0)
    n_sel = min(N_SEL, ns)
    _, sel_idx = lax.top_k(imp, n_sel)

    ks_blk = ks.reshape(B, ns, SEL_L, N_KV, HEAD_DIM).transpose(0, 3, 1, 2, 4)
    vs_blk = vs.reshape(B, ns, SEL_L, N_KV, HEAD_DIM).transpose(0, 3, 1, 2, 4)
    pad = ((0, 0), (WINDOW, 0), (0, 0), (0, 0))
    kw_pad = jnp.pad(kw, pad)
    vw_pad = jnp.pad(vw, pad)
    gather = jax.vmap(jax.vmap(lambda blk, ids: blk[ids]))

    def block_fn(qb):
        start = qb * Q_BLOCK
        q_b = lax.dynamic_slice_in_dim(qg, start, Q_BLOCK, 1)
        tq = start + jnp.arange(Q_BLOCK)
        ids = lax.dynamic_slice_in_dim(sel_idx, start, Q_BLOCK, 1).transpose(0, 2, 1, 3)
        ids_flat = ids.reshape(B, N_KV, Q_BLOCK * n_sel)
        k_sel = gather(ks_blk, ids_flat).reshape(B, N_KV, Q_BLOCK, n_sel * SEL_L, HEAD_DIM)
        v_sel = gather(vs_blk, ids_flat).reshape(B, N_KV, Q_BLOCK, n_sel * SEL_L, HEAD_DIM)
        kpos = (ids[..., None] * SEL_L + jnp.arange(SEL_L)).reshape(B, N_KV, Q_BLOCK, n_sel * SEL_L)
        m_sel = (kpos <= tq[None, None, :, None]).transpose(0, 2, 1, 3)[:, :, :, None, :]
        s_sel = jnp.einsum('bqghd,bgqkd->bqghk', q_b, k_sel).astype(jnp.float32) * scale
        p_sel = masked_softmax(s_sel, m_sel)
        o_sel = jnp.einsum('bqghk,bgqkd->bqghd', p_sel.astype(q.dtype), v_sel)
        k_w = lax.dynamic_slice_in_dim(kw_pad, start, Q_BLOCK + WINDOW, 1)
        v_w = lax.dynamic_slice_in_dim(vw_pad, start, Q_BLOCK + WINDOW, 1)
        kpos_w = start - WINDOW + jnp.arange(Q_BLOCK + WINDOW)
        m_w = ((kpos_w[None, :] <= tq[:, None]) & (kpos_w[None, :] > tq[:, None] - WINDOW)
               & (kpos_w[None, :] >= 0))
        s_w = jnp.einsum('bqghd,bkgd->bqghk', q_b, k_w).astype(jnp.float32) * scale
        p_w = masked_softmax(s_w, m_w[None, :, None, None, :])
        o_w = jnp.einsum('bqghk,bkgd->bqghd', p_w.astype(q.dtype), v_w)
        return o_sel, o_w

    o_sel, o_w = lax.map(block_fn, jnp.arange(S // Q_BLOCK))
    o_sel = jnp.swapaxes(o_sel, 0, 1).reshape(B, S, N_KV, HPG, HEAD_DIM)
    o_w = jnp.swapaxes(o_w, 0, 1).reshape(B, S, N_KV, HPG, HEAD_DIM)
    g = jax.nn.sigmoid(gate_logits).reshape(B, S, 3, N_KV, HPG)[..., None]
    o = g[:, :, 0] * o_cmp + g[:, :, 1] * o_sel + g[:, :, 2] * o_w
    return o.reshape(B, S, ATTN_W)


def setup_inputs(seed: int = 0) -> dict:
    key = jax.random.key(seed)
    ks = jax.random.split(key, 24)
    nrm = lambda k, shape, s: jax.random.normal(k, shape, jnp.float32) * s
    offs = jax.random.randint(ks[2], (BATCH, 1), 0, POS_OFFSET_MAX, dtype=jnp.int32)
    positions = (offs + jnp.arange(SEQ, dtype=jnp.int32)[None, :]).astype(jnp.int32)
    return {
        "x": nrm(ks[0], (BATCH, SEQ, D_MODEL), 1.0),
        "c": nrm(ks[1], (BATCH, D_MODEL), 1.0),
        "positions": positions,
        "w_ada": nrm(ks[3], (DEPTH, D_MODEL, 6 * D_MODEL), D_MODEL ** -0.5),
        "b_ada": nrm(ks[4], (DEPTH, 6 * D_MODEL), 0.01),
        "g_mix": 1.0 + nrm(ks[5], (DEPTH, D_MODEL), 0.02),
        "w_in": nrm(ks[6], (DEPTH, D_MODEL, IN_W), D_MODEL ** -0.5),
        "b_in": nrm(ks[7], (DEPTH, IN_W), 0.01),
        "conv_w": nrm(ks[8], (DEPTH, CONV_K, CONV_W), CONV_K ** -0.5),
        "cmp_k_w1": nrm(ks[9], (DEPTH, CMP_L * HEAD_DIM, CMP_HIDDEN), (CMP_L * HEAD_DIM) ** -0.5),
        "cmp_k_w2": nrm(ks[10], (DEPTH, CMP_HIDDEN, HEAD_DIM), CMP_HIDDEN ** -0.5),
        "cmp_k_pe": nrm(ks[11], (DEPTH, CMP_L, HEAD_DIM), 0.1),
        "cmp_v_w1": nrm(ks[12], (DEPTH, CMP_L * HEAD_DIM, CMP_HIDDEN), (CMP_L * HEAD_DIM) ** -0.5),
        "cmp_v_w2": nrm(ks[13], (DEPTH, CMP_HIDDEN, HEAD_DIM), CMP_HIDDEN ** -0.5),
        "cmp_v_pe": nrm(ks[14], (DEPTH, CMP_L, HEAD_DIM), 0.1),
        "w_up_conv": nrm(ks[15], (DEPTH, CONV_W, D_MODEL), CONV_W ** -0.5),
        "w_up_attn": nrm(ks[16], (DEPTH, ATTN_W, D_MODEL), ATTN_W ** -0.5),
        "w_o": nrm(ks[17], (DEPTH, D_MODEL, D_MODEL), D_MODEL ** -0.5),
        "g_mlp": 1.0 + nrm(ks[18], (DEPTH, D_MODEL), 0.02),
        "w_ff1": nrm(ks[19], (DEPTH, D_MODEL, D_FF), D_MODEL ** -0.5),
        "w_ff2": nrm(ks[20], (DEPTH, D_FF, D_MODEL), D_FF ** -0.5),
        "g_final": 1.0 + nrm(ks[21], (D_MODEL,), 0.02),
    }


def reference(x, c, positions, w_ada, b_ada, g_mix, w_in, b_in, conv_w, cmp_k_w1, cmp_k_w2, cmp_k_pe,
              cmp_v_w1, cmp_v_w2, cmp_v_pe, w_up_conv, w_up_attn, w_o, g_mlp, w_ff1, w_ff2, g_final):
    B, S, _ = x.shape
    cos, sin = rope_tables(positions)
    split_at = [int(v) for v in np.cumsum(IN_SIZES)[:-1]]
    c_act = jax.nn.silu(c)
    for l in range(DEPTH):
        cond = c_act @ w_ada[l] + b_ada[l]
        sh_a, sc_a, ga_a, sh_m, sc_m, ga_m = [v[:, None, :] for v in jnp.split(cond, 6, axis=-1)]

        h = rmsnorm(x, g_mix[l]) * (1.0 + sc_a) + sh_a
        z = h @ w_in[l] + b_in[l]
        (cb, cc, cu, q, kc, vc, k_s, v_s, k_w, v_w, nsa_g, merge_g) = jnp.split(z, split_at, axis=-1)
        y_conv = short_conv_mixer(cb, cc, cu, conv_w[l]) @ w_up_conv[l]
        q = apply_rope(q.reshape(B, S, N_HEADS, HEAD_DIM), cos, sin)
        kv = lambda a: a.reshape(B, S, N_KV, HEAD_DIM)
        kc = apply_rope(kv(kc), cos, sin)
        k_s = apply_rope(kv(k_s), cos, sin)
        k_w = apply_rope(kv(k_w), cos, sin)
        y_attn = nsa_mixer(q, kc, kv(vc), k_s, kv(v_s), k_w, kv(v_w), nsa_g,
                           cmp_k_w1[l], cmp_k_w2[l], cmp_k_pe[l],
                           cmp_v_w1[l], cmp_v_w2[l], cmp_v_pe[l]) @ w_up_attn[l]
        g_conv, g_attn = jnp.split(jax.nn.sigmoid(merge_g), 2, axis=-1)
        mixed = (g_conv * y_conv + g_attn * y_attn) @ w_o[l]
        x = x + ga_a * mixed

        h = rmsnorm(x, g_mlp[l]) * (1.0 + sc_m) + sh_m
        x = x + ga_m * (jnp.square(jax.nn.relu(h @ w_ff1[l])) @ w_ff2[l])
    return rmsnorm(x, g_final)
```

```python
import functools

import jax
import jax.numpy as jnp
import numpy as np
from jax import lax
from jax.experimental import pallas as pl
from jax.experimental.pallas import tpu as pltpu

D_MODEL = 1024
CONV_W = 512
CONV_K = 3
N_HEADS = 8
HEAD_DIM = 64
N_KV = 2
HPG = N_HEADS // N_KV
ATTN_W = N_HEADS * HEAD_DIM
KV_W = N_KV * HEAD_DIM
CMP_L = 32
CMP_D = 16
CMP_HIDDEN = 4 * HEAD_DIM
SEL_L = 64
N_SEL = 8
WINDOW = 512
FORCE_BONUS = 1e4
NEG = -1e30
D_FF = 4 * D_MODEL
ROPE_THETA = 10000.0
EPS = 1e-6
IN_SIZES = (CONV_W, CONV_W, CONV_W, ATTN_W, KV_W, KV_W, KV_W, KV_W, KV_W, KV_W, 3 * N_HEADS, 2 * D_MODEL)

LANE = 128
Q_PAD_W = N_HEADS * LANE
SEL_BIAS = -1e9
VMEM_LIMIT = 56 * 1024 * 1024

_C_CB, _C_CC, _C_CU = 0, CONV_W, 2 * CONV_W
_C_Q = 3 * CONV_W
_C_KC = _C_Q + Q_PAD_W
_C_VC = _C_KC + KV_W
_C_VS = _C_VC + KV_W
_C_VW = _C_VS + KV_W
_C_NG = _C_VW + KV_W
N_MAIN = _C_NG + LANE

F32 = jnp.float32
BF16 = jnp.bfloat16


def _cparams(n_grid):
    return pltpu.CompilerParams(dimension_semantics=("arbitrary",) * n_grid,
                                vmem_limit_bytes=VMEM_LIMIT)


def _dot(a, b):
    return jnp.dot(a, b, preferred_element_type=F32)


def _dot_nt(a, b):
    return lax.dot_general(a, b, (((1,), (1,)), ((), ())), preferred_element_type=F32)


def _modulated_norm(x, g, shift, scale):
    ms = jnp.mean(x * x, axis=-1, keepdims=True)
    y = x * lax.rsqrt(ms + EPS) * g
    return y * (1.0 + scale) + shift


def _ada_kernel(c_ref, w_ref, b_ref, o_ref):
    c = c_ref[...]
    ca = (c * jax.nn.sigmoid(c)).astype(BF16)
    o_ref[0] = _dot(ca, w_ref[0].astype(BF16)) + b_ref[0]


def _ada_call(c, w_ada, b_ada):
    depth, d, n = w_ada.shape
    bsz = c.shape[0]
    tn = 512
    return pl.pallas_call(
        _ada_kernel,
        grid=(depth, n // tn),
        in_specs=[pl.BlockSpec((bsz, d), lambda l, j: (0, 0)),
                  pl.BlockSpec((1, d, tn), lambda l, j: (l, 0, j)),
                  pl.BlockSpec((1, 1, tn), lambda l, j: (l, 0, j))],
        out_specs=pl.BlockSpec((1, bsz, tn), lambda l, j: (l, 0, j)),
        out_shape=jax.ShapeDtypeStruct((depth, bsz, n), F32),
        compiler_params=_cparams(2),
        name="ada_cond",
    )(c, w_ada, b_ada.reshape(depth, 1, n))


def _in_kernel(x_ref, mod_ref, g_ref, w_ref, b_ref, wkt_ref, bkt_ref, rope_ref, ropet_ref, cw_ref,
               uconv_ref, q_ref, kcvc_ref, vsw_ref, kt_ref, gsig_ref, carry_ref):
    i = pl.program_id(1)
    x = x_ref[0]
    tm = x.shape[0]
    mod = mod_ref[0]
    hb = _modulated_norm(x, g_ref[...], mod[0:1], mod[1:2]).astype(BF16)

    def seg(a, b):
        return _dot(hb, w_ref[:, a:b]) + b_ref[:, a:b]

    @pl.when(i == 0)
    def _():
        carry_ref[...] = jnp.zeros_like(carry_ref)

    v = seg(_C_CC, _C_CU) * seg(_C_CU, _C_Q)
    prev = carry_ref[...]
    row = lax.broadcasted_iota(jnp.int32, v.shape, 0)
    p1 = jnp.broadcast_to(prev[7:8], v.shape)
    p2 = jnp.broadcast_to(prev[6:7], v.shape)
    v1 = jnp.where(row == 0, p1, pltpu.roll(v, 1, 0))
    v2 = jnp.where(row == 0, p2, jnp.where(row == 1, p1, pltpu.roll(v, 2, 0)))
    cw = cw_ref[...]
    yc = cw[0:1] * v2 + cw[1:2] * v1 + cw[2:3] * v
    carry_ref[...] = v[tm - 8:tm]
    uconv_ref[0] = (seg(_C_CB, _C_CC) * yc).astype(BF16)

    cos2, sa2, sb2 = rope_ref[0, 0], rope_ref[0, 1], rope_ref[0, 2]
    lane = lax.broadcasted_iota(jnp.int32, cos2.shape, 1)
    low = lane < HEAD_DIM
    cos1, sa1, sb1 = jnp.where(low, cos2, 0.0), jnp.where(low, sa2, 0.0), jnp.where(low, sb2, 0.0)

    def rope(z, c, sa, sb):
        return z * c + pltpu.roll(z, LANE - HEAD_DIM // 2, 1) * sa + pltpu.roll(z, HEAD_DIM // 2, 1) * sb

    scale = HEAD_DIM ** -0.5
    for h in range(N_HEADS):
        zq = seg(_C_Q + h * LANE, _C_Q + (h + 1) * LANE)
        q_ref[0, :, h * LANE:(h + 1) * LANE] = (rope(zq, cos1, sa1, sb1) * scale).astype(BF16)

    kcvc_ref[0, :, 0:KV_W] = rope(seg(_C_KC, _C_VC), cos2, sa2, sb2).astype(BF16)
    kcvc_ref[0, :, KV_W:2 * KV_W] = seg(_C_VC, _C_VS).astype(BF16)
    vsw_ref[0] = seg(_C_VS, _C_NG).astype(BF16)
    gsig_ref[0] = jax.nn.sigmoid(seg(_C_NG, N_MAIN))

    zt = _dot_nt(wkt_ref[...], hb) + bkt_ref[...]
    ct, st = ropet_ref[0, 0], ropet_ref[0, 1]
    half = HEAD_DIM // 2
    for r in range(2 * N_KV):
        x1 = zt[r * HEAD_DIM:r * HEAD_DIM + half]
        x2 = zt[r * HEAD_DIM + half:(r + 1) * HEAD_DIM]
        kt_ref[0, r, 0:half, :] = (x1 * ct - x2 * st).astype(BF16)
        kt_ref[0, r, half:HEAD_DIM, :] = (x2 * ct + x1 * st).astype(BF16)


def _in_call(x, mod, g, w_main, b_main, w_kt, b_kt, rope_tab, rope_tab_t, conv_w, tm):
    bsz, s, d = x.shape
    const = lambda *shape: pl.BlockSpec(shape, lambda b, i: (0,) * len(shape))
    return pl.pallas_call(
        _in_kernel,
        grid=(bsz, s // tm),
        in_specs=[pl.BlockSpec((1, tm, d), lambda b, i: (b, i, 0)),
                  pl.BlockSpec((1, 6, d), lambda b, i: (b, 0, 0)),
                  const(1, d),
                  const(d, N_MAIN),
                  const(1, N_MAIN),
                  const(2 * KV_W, d),
                  const(2 * KV_W, 1),
                  pl.BlockSpec((1, 3, tm, LANE), lambda b, i: (b, 0, i, 0)),
                  pl.BlockSpec((1, 2, HEAD_DIM // 2, tm), lambda b, i: (b, 0, 0, i)),
                  const(CONV_K, CONV_W)],
        out_specs=[pl.BlockSpec((1, tm, CONV_W), lambda b, i: (b, i, 0)),
                   pl.BlockSpec((1, tm, Q_PAD_W), lambda b, i: (b, i, 0)),
                   pl.BlockSpec((1, tm, 2 * KV_W), lambda b, i: (b, i, 0)),
                   pl.BlockSpec((1, tm, 2 * KV_W), lambda b, i: (b, i, 0)),
                   pl.BlockSpec((1, 2 * N_KV, HEAD_DIM, tm), lambda b, i: (b, 0, 0, i)),
                   pl.BlockSpec((1, tm, LANE), lambda b, i: (b, i, 0))],
        out_shape=[jax.ShapeDtypeStruct((bsz, s, CONV_W), BF16),
                   jax.ShapeDtypeStruct((bsz, s, Q_PAD_W), BF16),
                   jax.ShapeDtypeStruct((bsz, s, 2 * KV_W), BF16),
                   jax.ShapeDtypeStruct((bsz, s, 2 * KV_W), BF16),
                   jax.ShapeDtypeStruct((bsz, 2 * N_KV, HEAD_DIM, s), BF16),
                   jax.ShapeDtypeStruct((bsz, s, LANE), F32)],
        scratch_shapes=[pltpu.VMEM((8, CONV_W), F32)],
        compiler_params=_cparams(2),
        name="in_proj",
    )(x, mod, g, w_main, b_main, w_kt, b_kt, rope_tab, rope_tab_t, conv_w)


def _cmp_kernel(uk_ref, uv_ref, w1k_ref, w1v_ref, pek_ref, pev_ref, w2kt_ref, w2v_ref,
                kct_ref, vcm_ref, *, nb, nchunk):
    m = uk_ref.shape[0]
    kh = w1k_ref.shape[0] // 2

    def hidden(u_ref, w1_ref, pe_ref):
        u = u_ref[...]
        pe = pe_ref[...]
        pe_hi = pe.astype(BF16)
        pe_lo = (pe - pe_hi.astype(F32)).astype(BF16)
        w1 = w1_ref[...]
        pterm = (_dot(pe_hi, w1) + _dot(pe_lo, w1))[0:1]
        a = _dot(u, w1[0:kh])
        bm = _dot(u, w1[kh:2 * kh])
        hid = a + pltpu.roll(bm, m - 1, 0) + pterm
        return (hid * jax.nn.sigmoid(hid)).astype(BF16)

    hk = hidden(uk_ref, w1k_ref, pek_ref)
    hv = hidden(uv_ref, w1v_ref, pev_ref)
    for j in range(nb):
        acc = None
        for g in range(N_KV):
            r0 = (j * N_KV + g) * nchunk
            kct_ref[j, g] = _dot_nt(w2kt_ref[...], hk[r0:r0 + nchunk]).astype(BF16)
            part = _dot(hv[r0:r0 + nchunk], w2v_ref[g])
            acc = part if acc is None else acc + part
        vcm_ref[j] = acc.astype(BF16)


def _cmp_call(uk, uv, w1k, w1v, pek, pev, w2kt, w2v, bsz, nchunk, nb):
    m = nb * N_KV * nchunk
    feat = uk.shape[1]
    const = lambda *shape: pl.BlockSpec(shape, lambda j: (0,) * len(shape))
    return pl.pallas_call(
        functools.partial(_cmp_kernel, nb=nb, nchunk=nchunk),
        grid=(bsz // nb,),
        in_specs=[pl.BlockSpec((m, feat), lambda j: (j, 0)),
                  pl.BlockSpec((m, feat), lambda j: (j, 0)),
                  const(2 * feat, CMP_HIDDEN),
                  const(2 * feat, CMP_HIDDEN),
                  const(8, 2 * feat),
                  const(8, 2 * feat),
                  const(LANE, CMP_HIDDEN),
                  const(N_KV, CMP_HIDDEN, LANE)],
        out_specs=[pl.BlockSpec((nb, N_KV, LANE, nchunk), lambda j: (j, 0, 0, 0)),
                   pl.BlockSpec((nb, nchunk, LANE), lambda j: (j, 0, 0))],
        out_shape=[jax.ShapeDtypeStruct((bsz, N_KV, LANE, nchunk), BF16),
                   jax.ShapeDtypeStruct((bsz, nchunk, LANE), BF16)],
        compiler_params=_cparams(1),
        name="cmp_mlp",
    )(uk, uv, w1k, w1v, pek, pev, w2kt, w2v)


def _attn_kernel(q_ref, kct_ref, vcm_ref, kst_ref, kwt_ref, vs_ref, vw_ref, gsig_ref, ov_ref, e_ref,
                 o_ref, m_s, l_s, acc_s, ocmp_s, osel_s, *, tq, ns, n_sel):
    g = pl.program_id(1)
    i = pl.program_id(2)
    rows = HPG * tq
    qst = jnp.concatenate([q_ref[0, :, h * LANE:(h + 1) * LANE] for h in range(HPG)], axis=0)

    row_q = lax.broadcasted_iota(jnp.int32, (rows, 1), 0) & (tq - 1)
    t_row = i * tq + row_q

    nct = kct_ref.shape[-1]
    s = _dot(qst, kct_ref[0, 0])
    col = lax.broadcasted_iota(jnp.int32, (rows, nct), 1)
    cmask = (col * CMP_D + (CMP_L - 1)) <= t_row
    s = jnp.where(cmask, s, NEG)
    mx = jnp.max(s, axis=-1, keepdims=True)
    e = jnp.where(cmask, jnp.exp(s - mx), 0.0)
    den = jnp.sum(e, axis=-1, keepdims=True)
    p = e / jnp.where(den > 0.0, den, 1.0)
    ocmp_s[...] = _dot(p.astype(BF16), vcm_ref[0])

    psum = p[0:tq]
    for h in range(1, HPG):
        psum = psum + p[h * tq:(h + 1) * tq]
    p_hi = psum.astype(BF16)
    p_lo = (psum - p_hi.astype(F32)).astype(BF16)
    imp = _dot(p_hi, ov_ref[...]) + _dot(p_lo, ov_ref[...])
    lane = lax.broadcasted_iota(jnp.int32, (tq, LANE), 1)
    lane_f = lane.astype(F32)
    tt = i * tq + lax.broadcasted_iota(jnp.int32, (tq, LANE), 0)
    bt = tt // SEL_L
    j = lane - HEAD_DIM
    in_range = (j >= 0) & (j < ns)
    forced = (j == 0) | (j == bt) | (j == bt - 1)
    val = jnp.where(j <= bt, imp + jnp.where(forced, FORCE_BONUS, 0.0), -1.0)
    val = jnp.where(in_range, val, -3.0)
    sel = jnp.zeros((tq, LANE), F32)
    for _ in range(n_sel):
        top = jnp.max(val, axis=-1, keepdims=True)
        first = jnp.min(jnp.where(val == top, lane_f, 1e3), axis=-1, keepdims=True)
        hit = lane_f == first
        sel = jnp.where(hit, 1.0, sel)
        val = jnp.where(hit, -5.0, val)
    bias = jnp.where(in_range & (sel == 0.0), SEL_BIAS, 0.0).astype(BF16)
    bias_st = jnp.concatenate([bias] * HPG, axis=0)
    lane_r = lax.broadcasted_iota(jnp.int32, (rows, LANE), 1)
    qsel = jnp.where(lane_r < HEAD_DIM, qst, bias_st)

    col_k = lax.broadcasted_iota(jnp.int32, (rows, tq), 1)
    causal = col_k <= row_q
    band = col_k > row_q

    def reset():
        m_s[...] = jnp.full_like(m_s, NEG)
        l_s[...] = jnp.zeros_like(l_s)
        acc_s[...] = jnp.zeros_like(acc_s)

    def update(sc, v_tile):
        m_old = m_s[...]
        m_new = jnp.maximum(m_old, jnp.max(sc, axis=-1, keepdims=True))
        alpha = jnp.exp(m_old - m_new)
        pt = jnp.exp(sc - m_new)
        l_s[...] = alpha * l_s[...] + jnp.sum(pt, axis=-1, keepdims=True)
        acc_s[...] = alpha * acc_s[...] + _dot(pt.astype(BF16), v_tile)
        m_s[...] = m_new

    def sel_tile(kt, mask):
        k0 = pl.multiple_of(kt * tq, tq)
        kaug = jnp.concatenate([kst_ref[0, 0, :, pl.ds(k0, tq)], e_ref[:, pl.ds(k0, tq)]], axis=0)
        sc = _dot(qsel, kaug)
        if mask is not None:
            sc = jnp.where(mask, sc, NEG)
        update(sc, vs_ref[0, pl.ds(k0, tq), :])

    reset()

    def sel_body(kt, carry):
        sel_tile(kt, None)
        return carry

    lax.fori_loop(0, i, sel_body, 0)
    sel_tile(i, causal)
    osel_s[...] = acc_s[...] / l_s[...]

    def win_tile(kt, mask):
        k0 = pl.multiple_of(kt * tq, tq)
        sc = _dot(qst, jnp.concatenate([kwt_ref[0, 0, :, pl.ds(k0, tq)],
                                        jnp.zeros((LANE - HEAD_DIM, tq), BF16)], axis=0))
        if mask is not None:
            sc = jnp.where(mask, sc, NEG)
        update(sc, vw_ref[0, pl.ds(k0, tq), :])

    reset()
    n_back = WINDOW // tq
    for back in range(n_back, 0, -1):
        @pl.when(i >= back)
        def _(back=back):
            win_tile(i - back, band if back == n_back else None)
    win_tile(i, causal)
    o_win = acc_s[...] / l_s[...]

    gs = gsig_ref[0]
    o_cmp = ocmp_s[...]
    o_sel = osel_s[...]

    def gate(branch, h):
        idx = branch * N_HEADS + g * HPG + h
        return jnp.sum(jnp.where(lane == idx, gs, 0.0), axis=-1, keepdims=True)

    for h in range(HPG):
        r0, r1 = h * tq, (h + 1) * tq
        o_h = gate(0, h) * o_cmp[r0:r1] + gate(1, h) * o_sel[r0:r1] + gate(2, h) * o_win[r0:r1]
        o_ref[0, :, h * LANE:(h + 1) * LANE] = o_h.astype(BF16)


def _attn_call(q, kct, vcm, kt, vsw, gsig, ov, e_mat, tq):
    bsz, s, _ = q.shape
    ns = s // SEL_L
    rows = HPG * tq
    nct = kct.shape[-1]
    return pl.pallas_call(
        functools.partial(_attn_kernel, tq=tq, ns=ns, n_sel=min(N_SEL, ns)),
        grid=(bsz, N_KV, s // tq),
        in_specs=[pl.BlockSpec((1, tq, HPG * LANE), lambda b, g, i: (b, i, g)),
                  pl.BlockSpec((1, 1, LANE, nct), lambda b, g, i: (b, g, 0, 0)),
                  pl.BlockSpec((1, nct, LANE), lambda b, g, i: (b, 0, 0)),
                  pl.BlockSpec((1, 1, HEAD_DIM, s), lambda b, g, i: (b, g, 0, 0)),
                  pl.BlockSpec((1, 1, HEAD_DIM, s), lambda b, g, i: (b, N_KV + g, 0, 0)),
                  pl.BlockSpec((1, s, LANE), lambda b, g, i: (b, 0, 0)),
                  pl.BlockSpec((1, s, LANE), lambda b, g, i: (b, 0, 1)),
                  pl.BlockSpec((1, tq, LANE), lambda b, g, i: (b, i, 0)),
                  pl.BlockSpec((nct, LANE), lambda b, g, i: (0, 0)),
                  pl.BlockSpec((LANE - HEAD_DIM, s), lambda b, g, i: (0, 0))],
        out_specs=pl.BlockSpec((1, tq, HPG * LANE), lambda b, g, i: (b, i, g)),
        out_shape=jax.ShapeDtypeStruct((bsz, s, Q_PAD_W), BF16),
        scratch_shapes=[pltpu.VMEM((rows, 1), F32), pltpu.VMEM((rows, 1), F32),
                        pltpu.VMEM((rows, LANE), F32), pltpu.VMEM((rows, LANE), F32),
                        pltpu.VMEM((rows, LANE), F32)],
        compiler_params=_cparams(3),
        name="nsa_attn",
    )(q, kct, vcm, kt, kt, vsw, vsw, gsig, ov, e_mat)


def _merge_kernel(x_ref, mod_ref, g_ref, uconv_ref, o_ref, wmg_ref, bmg_ref, wuc_ref, wua_ref, wo_ref, out_ref):
    x = x_ref[0]
    mod = mod_ref[0]
    d = x.shape[-1]
    hb = _modulated_norm(x, g_ref[...], mod[0:1], mod[1:2]).astype(BF16)
    y_conv = _dot(uconv_ref[0], wuc_ref[...])
    y_attn = _dot(o_ref[0], wua_ref[...])
    g_conv = jax.nn.sigmoid(_dot(hb, wmg_ref[:, 0:d]) + bmg_ref[:, 0:d])
    g_attn = jax.nn.sigmoid(_dot(hb, wmg_ref[:, d:2 * d]) + bmg_ref[:, d:2 * d])
    mixed = _dot((g_conv * y_conv + g_attn * y_attn).astype(BF16), wo_ref[...])
    out_ref[0] = x + mod[2:3] * mixed


def _merge_call(x, mod, g, uconv, o, w_mg, b_mg, w_uc, w_ua, w_o, tm):
    bsz, s, d = x.shape
    const = lambda *shape: pl.BlockSpec(shape, lambda b, i: (0,) * len(shape))
    return pl.pallas_call(
        _merge_kernel,
        grid=(bsz, s // tm),
        in_specs=[pl.BlockSpec((1, tm, d), lambda b, i: (b, i, 0)),
                  pl.BlockSpec((1, 6, d), lambda b, i: (b, 0, 0)),
                  const(1, d),
                  pl.BlockSpec((1, tm, CONV_W), lambda b, i: (b, i, 0)),
                  pl.BlockSpec((1, tm, Q_PAD_W), lambda b, i: (b, i, 0)),
                  const(d, 2 * d), const(1, 2 * d), const(CONV_W, d), const(Q_PAD_W, d), const(d, d)],
        out_specs=pl.BlockSpec((1, tm, d), lambda b, i: (b, i, 0)),
        out_shape=jax.ShapeDtypeStruct((bsz, s, d), F32),
        compiler_params=_cparams(2),
        name="merge_out",
    )(x, mod, g, uconv, o, w_mg, b_mg, w_uc, w_ua, w_o)


def _mlp_kernel(x_ref, mod_ref, g_ref, w1_ref, w2_ref, out_ref, *, tf):
    x = x_ref[0]
    mod = mod_ref[0]
    hb = _modulated_norm(x, g_ref[...], mod[3:4], mod[4:5]).astype(BF16)
    acc = None
    for c in range(w1_ref.shape[1] // tf):
        a = jnp.maximum(_dot(hb, w1_ref[:, c * tf:(c + 1) * tf]), 0.0)
        part = _dot((a * a).astype(BF16), w2_ref[c * tf:(c + 1) * tf, :])
        acc = part if acc is None else acc + part
    out_ref[0] = x + mod[5:6] * acc


def _mlp_call(x, mod, g, w1, w2, tm, tf):
    bsz, s, d = x.shape
    const = lambda *shape: pl.BlockSpec(shape, lambda b, i: (0,) * len(shape))
    return pl.pallas_call(
        functools.partial(_mlp_kernel, tf=tf),
        grid=(bsz, s // tm),
        in_specs=[pl.BlockSpec((1, tm, d), lambda b, i: (b, i, 0)),
                  pl.BlockSpec((1, 6, d), lambda b, i: (b, 0, 0)),
                  const(1, d), const(d, w1.shape[1]), const(w1.shape[1], d)],
        out_specs=pl.BlockSpec((1, tm, d), lambda b, i: (b, i, 0)),
        out_shape=jax.ShapeDtypeStruct((bsz, s, d), F32),
        compiler_params=_cparams(2),
        name="relu2_mlp",
    )(x, mod, g, w1, w2)


def _final_kernel(x_ref, g_ref, out_ref):
    x = x_ref[0]
    ms = jnp.mean(x * x, axis=-1, keepdims=True)
    out_ref[0] = x * lax.rsqrt(ms + EPS) * g_ref[...]


def _final_call(x, g, tm):
    bsz, s, d = x.shape
    return pl.pallas_call(
        _final_kernel,
        grid=(bsz, s // tm),
        in_specs=[pl.BlockSpec((1, tm, d), lambda b, i: (b, i, 0)),
                  pl.BlockSpec((1, d), lambda b, i: (0, 0))],
        out_specs=pl.BlockSpec((1, tm, d), lambda b, i: (b, i, 0)),
        out_shape=jax.ShapeDtypeStruct((bsz, s, d), F32),
        compiler_params=_cparams(2),
        name="final_norm",
    )(x, g)


def _pack_weights(w_in, b_in, cmp_k_w2, cmp_v_w2, w_up_attn):
    depth, d, _ = w_in.shape
    offs = np.concatenate([[0], np.cumsum(IN_SIZES)])
    col = lambda a, k: a[..., int(offs[k]):int(offs[k + 1])]

    def pad_heads(a):
        lead = a.shape[:-1]
        a = a.reshape(lead + (N_HEADS, HEAD_DIM))
        a = jnp.pad(a, [(0, 0)] * len(lead) + [(0, 0), (0, LANE - HEAD_DIM)])
        return a.reshape(lead + (Q_PAD_W,))

    def pad_lanes(a):
        return jnp.pad(a, [(0, 0)] * (a.ndim - 1) + [(0, LANE - a.shape[-1])])

    def main(a):
        return jnp.concatenate([col(a, 0), col(a, 1), col(a, 2), pad_heads(col(a, 3)), col(a, 4), col(a, 5),
                                col(a, 7), col(a, 9), pad_lanes(col(a, 10))], axis=-1)

    w_main = main(w_in).astype(BF16)
    b_main = main(b_in)[:, None, :]
    w_kt = jnp.swapaxes(jnp.concatenate([col(w_in, 6), col(w_in, 8)], axis=-1), 1, 2).astype(BF16)
    b_kt = jnp.concatenate([col(b_in, 6), col(b_in, 8)], axis=-1)[:, :, None]
    w_mg = col(w_in, 11).astype(BF16)
    b_mg = col(b_in, 11)[:, None, :]

    w2kt = jnp.pad(jnp.swapaxes(cmp_k_w2, 1, 2), [(0, 0), (0, LANE - HEAD_DIM), (0, 0)]).astype(BF16)
    w2v = jnp.stack([jnp.pad(cmp_v_w2, [(0, 0), (0, 0), (gi * HEAD_DIM, LANE - (gi + 1) * HEAD_DIM)])
                     for gi in range(N_KV)], axis=1).astype(BF16)

    wa = w_up_attn.reshape(depth, N_HEADS, HEAD_DIM, d)
    slots = []
    for hh in range(N_HEADS):
        lo = (hh // HPG) * HEAD_DIM
        slots.append(jnp.pad(wa[:, hh], [(0, 0), (lo, LANE - HEAD_DIM - lo), (0, 0)]))
    w_ua = jnp.concatenate(slots, axis=1).astype(BF16)
    return w_main, b_main, w_kt, b_kt, w_mg, b_mg, w2kt, w2v, w_ua


def _rope_tables(positions):
    half = HEAD_DIM // 2
    inv_freq = ROPE_THETA ** (-jnp.arange(half, dtype=F32) * 2.0 / HEAD_DIM)
    ang = positions.astype(F32)[..., None] * inv_freq
    cos, sin = jnp.cos(ang), jnp.sin(ang)
    zero = jnp.zeros_like(sin)
    reps = LANE // HEAD_DIM
    lane_tab = jnp.stack([jnp.tile(cos, (1, 1, 2 * reps)),
                          jnp.tile(jnp.concatenate([-sin, zero], axis=-1), (1, 1, reps)),
                          jnp.tile(jnp.concatenate([zero, sin], axis=-1), (1, 1, reps))], axis=1)
    t_tab = jnp.stack([jnp.swapaxes(cos, 1, 2), jnp.swapaxes(sin, 1, 2)], axis=1)
    return lane_tab, t_tab


def _static_tables(s, nchunk):
    ns = s // SEL_L
    cstart = np.arange(nchunk) * CMP_D
    jj = np.arange(ns)
    overlap = (cstart[:, None] < (jj[None, :] + 1) * SEL_L) & (cstart[:, None] + CMP_L > jj[None, :] * SEL_L)
    ov = np.zeros((nchunk, LANE), np.float32)
    ov[:, HEAD_DIM:HEAD_DIM + ns] = overlap
    e_mat = np.zeros((LANE - HEAD_DIM, s), np.float32)
    e_mat[np.arange(s) // SEL_L, np.arange(s)] = 1.0
    return jnp.asarray(ov, BF16), jnp.asarray(e_mat, BF16)


def kernel(x, c, positions, w_ada, b_ada, g_mix, w_in, b_in, conv_w, cmp_k_w1, cmp_k_w2, cmp_k_pe, cmp_v_w1, cmp_v_w2, cmp_v_pe, w_up_conv, w_up_attn, w_o, g_mlp, w_ff1, w_ff2, g_final):
    bsz, s, d = x.shape
    depth = w_ada.shape[0]
    nchunk = s // CMP_D
    tm = 512
    tq = 256
    nb = 4 if bsz % 4 == 0 else 1
    assert s % tm == 0 and s % tq == 0 and WINDOW % tq == 0 and nchunk == LANE and s // SEL_L <= LANE - HEAD_DIM

    w_main, b_main, w_kt, b_kt, w_mg, b_mg, w2kt, w2v, w_ua = _pack_weights(w_in, b_in, cmp_k_w2, cmp_v_w2, w_up_attn)
    w1k, w1v = cmp_k_w1.astype(BF16), cmp_v_w1.astype(BF16)
    pek = jnp.broadcast_to(cmp_k_pe.reshape(depth, 1, CMP_L * HEAD_DIM), (depth, 8, CMP_L * HEAD_DIM))
    pev = jnp.broadcast_to(cmp_v_pe.reshape(depth, 1, CMP_L * HEAD_DIM), (depth, 8, CMP_L * HEAD_DIM))
    w_uc, w_ob = w_up_conv.astype(BF16), w_o.astype(BF16)
    w_f1, w_f2 = w_ff1.astype(BF16), w_ff2.astype(BF16)
    rope_tab, rope_tab_t = _rope_tables(positions)
    ov, e_mat = _static_tables(s, nchunk)

    cond = _ada_call(c, w_ada, b_ada).reshape(depth, bsz, 6, d)

    def chunks(a):
        a = a.reshape(bsz, nchunk, CMP_D, N_KV, HEAD_DIM)
        return jnp.transpose(a, (0, 3, 1, 2, 4)).reshape(bsz * N_KV * nchunk, CMP_D * HEAD_DIM)

    for l in range(depth):
        mod = cond[l]
        uconv, q, kcvc, vsw, kt, gsig = _in_call(x, mod, g_mix[l][None], w_main[l], b_main[l], w_kt[l], b_kt[l],
                                                 rope_tab, rope_tab_t, conv_w[l], tm)
        kct, vcm = _cmp_call(chunks(kcvc[..., :KV_W]), chunks(kcvc[..., KV_W:]), w1k[l], w1v[l], pek[l], pev[l],
                             w2kt[l], w2v[l], bsz, nchunk, nb)
        o = _attn_call(q, kct, vcm, kt, vsw, gsig, ov, e_mat, tq)
        x = _merge_call(x, mod, g_mix[l][None], uconv, o, w_mg[l], b_mg[l], w_uc[l], w_ua[l], w_ob[l], tm)
        x = _mlp_call(x, mod, g_mlp[l][None], w_f1[l], w_f2[l], tm, 1024)
    return _final_call(x, g_final[None], tm)
```

```python
import functools

import jax
import jax.numpy as jnp
import numpy as np
from jax import lax
from jax.experimental import pallas as pl
from jax.experimental.pallas import tpu as pltpu

D_MODEL = 1024
CONV_W = 512
CONV_K = 3
N_HEADS = 8
HEAD_DIM = 64
N_KV = 2
HPG = N_HEADS // N_KV
ATTN_W = N_HEADS * HEAD_DIM
KV_W = N_KV * HEAD_DIM
CMP_L = 32
CMP_D = 16
CMP_HIDDEN = 4 * HEAD_DIM
SEL_L = 64
N_SEL = 8
WINDOW = 512
FORCE_BONUS = 1e4
NEG = -1e30
D_FF = 4 * D_MODEL
ROPE_THETA = 10000.0
EPS = 1e-6
IN_SIZES = (CONV_W, CONV_W, CONV_W, ATTN_W, KV_W, KV_W, KV_W, KV_W, KV_W, KV_W, 3 * N_HEADS, 2 * D_MODEL)

LANE = 128
SUBLANE = 8
SEL_BIAS = -1e9
VMEM_LIMIT = 56 * 1024 * 1024
GATE_ROWS = 3 * SUBLANE

_C_CB, _C_CC, _C_CU = 0, CONV_W, 2 * CONV_W
_C_KC = 3 * CONV_W
_C_VC = _C_KC + KV_W
_C_KS = _C_VC + KV_W
_C_KW = _C_KS + N_KV * LANE
N_MAIN = _C_KW + N_KV * LANE
_R_Q = 0
_R_VS = ATTN_W
_R_VW = _R_VS + KV_W
_R_G = _R_VW + KV_W
N_TRANS = _R_G + N_KV * GATE_ROWS

F32 = jnp.float32
BF16 = jnp.bfloat16


def _cparams(n_grid):
    return pltpu.CompilerParams(dimension_semantics=("arbitrary",) * n_grid,
                                vmem_limit_bytes=VMEM_LIMIT)


def _dot(a, b):
    return jnp.dot(a, b, preferred_element_type=F32)


def _dot_nt(a, b):
    return lax.dot_general(a, b, (((1,), (1,)), ((), ())), preferred_element_type=F32)


def _dot_tn(a, b):
    return lax.dot_general(a, b, (((0,), (0,)), ((), ())), preferred_element_type=F32)


def _modulated_norm(x, g, shift, scale):
    ms = jnp.mean(x * x, axis=-1, keepdims=True)
    y = x * lax.rsqrt(ms + EPS) * g
    return y * (1.0 + scale) + shift


def _ada_kernel(c_ref, w_ref, b_ref, o_ref):
    c = c_ref[...]
    ca = (c * jax.nn.sigmoid(c)).astype(BF16)
    o_ref[0] = _dot(ca, w_ref[0].astype(BF16)) + b_ref[0]


def _ada_call(c, w_ada, b_ada):
    depth, d, n = w_ada.shape
    bsz = c.shape[0]
    tn = 512
    return pl.pallas_call(
        _ada_kernel,
        grid=(depth, n // tn),
        in_specs=[pl.BlockSpec((bsz, d), lambda l, j: (0, 0)),
                  pl.BlockSpec((1, d, tn), lambda l, j: (l, 0, j)),
                  pl.BlockSpec((1, 1, tn), lambda l, j: (l, 0, j))],
        out_specs=pl.BlockSpec((1, bsz, tn), lambda l, j: (l, 0, j)),
        out_shape=jax.ShapeDtypeStruct((depth, bsz, n), F32),
        compiler_params=_cparams(2),
        name="ada_cond",
    )(c, w_ada, b_ada.reshape(depth, 1, n))


def _in_kernel(x_ref, mod_ref, g_ref, w_ref, b_ref, wt_ref, bt_ref, rope_ref, ropet_ref, cw_ref,
               uconv_ref, kcvc_ref, ksw_ref, qt_ref, vt_ref, gt_ref, carry_ref):
    i = pl.program_id(1)
    x = x_ref[0]
    tm = x.shape[0]
    mod = mod_ref[0]
    hb = _modulated_norm(x, g_ref[...], mod[0:1], mod[1:2]).astype(BF16)

    def seg(a, b):
        return _dot(hb, w_ref[:, a:b]) + b_ref[:, a:b]

    @pl.when(i == 0)
    def _():
        carry_ref[...] = jnp.zeros_like(carry_ref)

    v = seg(_C_CC, _C_CU) * seg(_C_CU, _C_KC)
    prev = carry_ref[...]
    row = lax.broadcasted_iota(jnp.int32, v.shape, 0)
    p1 = jnp.broadcast_to(prev[7:8], v.shape)
    p2 = jnp.broadcast_to(prev[6:7], v.shape)
    v1 = jnp.where(row == 0, p1, pltpu.roll(v, 1, 0))
    v2 = jnp.where(row == 0, p2, jnp.where(row == 1, p1, pltpu.roll(v, 2, 0)))
    cw = cw_ref[...]
    yc = cw[0:1] * v2 + cw[1:2] * v1 + cw[2:3] * v
    carry_ref[...] = v[tm - 8:tm]
    uconv_ref[0] = (seg(_C_CB, _C_CC) * yc).astype(BF16)

    cos2, sa2, sb2 = rope_ref[0, 0], rope_ref[0, 1], rope_ref[0, 2]
    lane = lax.broadcasted_iota(jnp.int32, cos2.shape, 1)
    low = lane < HEAD_DIM
    cos1, sa1, sb1 = jnp.where(low, cos2, 0.0), jnp.where(low, sa2, 0.0), jnp.where(low, sb2, 0.0)

    def rope(z, c, sa, sb):
        return z * c + pltpu.roll(z, LANE - HEAD_DIM // 2, 1) * sa + pltpu.roll(z, HEAD_DIM // 2, 1) * sb

    kcvc_ref[0, :, 0:KV_W] = rope(seg(_C_KC, _C_VC), cos2, sa2, sb2).astype(BF16)
    kcvc_ref[0, :, KV_W:2 * KV_W] = seg(_C_VC, _C_KS).astype(BF16)
    for r in range(2 * N_KV):
        zk = seg(_C_KS + r * LANE, _C_KS + (r + 1) * LANE)
        ksw_ref[0, r] = rope(zk, cos1, sa1, sb1).astype(BF16)

    zt = _dot_nt(wt_ref[...], hb) + bt_ref[...]
    ct, st = ropet_ref[0, 0], ropet_ref[0, 1]
    half = HEAD_DIM // 2
    scale = HEAD_DIM ** -0.5
    for h in range(N_HEADS):
        x1 = zt[_R_Q + h * HEAD_DIM:_R_Q + h * HEAD_DIM + half]
        x2 = zt[_R_Q + h * HEAD_DIM + half:_R_Q + (h + 1) * HEAD_DIM]
        qt_ref[0, h, 0:half, :] = ((x1 * ct - x2 * st) * scale).astype(BF16)
        qt_ref[0, h, half:HEAD_DIM, :] = ((x2 * ct + x1 * st) * scale).astype(BF16)
    for r in range(2 * N_KV):
        vt_ref[0, r] = zt[_R_VS + r * HEAD_DIM:_R_VS + (r + 1) * HEAD_DIM].astype(BF16)
    for gi in range(N_KV):
        gt_ref[0, gi] = jax.nn.sigmoid(zt[_R_G + gi * GATE_ROWS:_R_G + (gi + 1) * GATE_ROWS])


def _in_call(x, mod, g, w_main, b_main, w_t, b_t, rope_tab, rope_tab_t, conv_w, tm):
    bsz, s, d = x.shape
    const = lambda *shape: pl.BlockSpec(shape, lambda b, i: (0,) * len(shape))
    return pl.pallas_call(
        _in_kernel,
        grid=(bsz, s // tm),
        in_specs=[pl.BlockSpec((1, tm, d), lambda b, i: (b, i, 0)),
                  pl.BlockSpec((1, 6, d), lambda b, i: (b, 0, 0)),
                  const(1, d),
                  const(d, N_MAIN),
                  const(1, N_MAIN),
                  const(N_TRANS, d),
                  const(N_TRANS, 1),
                  pl.BlockSpec((1, 3, tm, LANE), lambda b, i: (b, 0, i, 0)),
                  pl.BlockSpec((1, 2, HEAD_DIM // 2, tm), lambda b, i: (b, 0, 0, i)),
                  const(CONV_K, CONV_W)],
        out_specs=[pl.BlockSpec((1, tm, CONV_W), lambda b, i: (b, i, 0)),
                   pl.BlockSpec((1, tm, 2 * KV_W), lambda b, i: (b, i, 0)),
                   pl.BlockSpec((1, 2 * N_KV, tm, LANE), lambda b, i: (b, 0, i, 0)),
                   pl.BlockSpec((1, N_HEADS, HEAD_DIM, tm), lambda b, i: (b, 0, 0, i)),
                   pl.BlockSpec((1, 2 * N_KV, HEAD_DIM, tm), lambda b, i: (b, 0, 0, i)),
                   pl.BlockSpec((1, N_KV, GATE_ROWS, tm), lambda b, i: (b, 0, 0, i))],
        out_shape=[jax.ShapeDtypeStruct((bsz, s, CONV_W), BF16),
                   jax.ShapeDtypeStruct((bsz, s, 2 * KV_W), BF16),
                   jax.ShapeDtypeStruct((bsz, 2 * N_KV, s, LANE), BF16),
                   jax.ShapeDtypeStruct((bsz, N_HEADS, HEAD_DIM, s), BF16),
                   jax.ShapeDtypeStruct((bsz, 2 * N_KV, HEAD_DIM, s), BF16),
                   jax.ShapeDtypeStruct((bsz, N_KV, GATE_ROWS, s), F32)],
        scratch_shapes=[pltpu.VMEM((8, CONV_W), F32)],
        compiler_params=_cparams(2),
        name="in_proj",
    )(x, mod, g, w_main, b_main, w_t, b_t, rope_tab, rope_tab_t, conv_w)


def _cmp_kernel(uk_ref, uv_ref, w1k_ref, w1v_ref, pek_ref, pev_ref, w2k_ref, w2vt_ref,
                kcn_ref, vct_ref, *, nb, nchunk):
    m = uk_ref.shape[0]
    kh = w1k_ref.shape[0] // 2

    def hidden(u_ref, w1_ref, pe_ref):
        u = u_ref[...]
        pe = pe_ref[...]
        pe_hi = pe.astype(BF16)
        pe_lo = (pe - pe_hi.astype(F32)).astype(BF16)
        w1 = w1_ref[...]
        pterm = (_dot(pe_hi, w1) + _dot(pe_lo, w1))[0:1]
        a = _dot(u, w1[0:kh])
        bm = _dot(u, w1[kh:2 * kh])
        hid = a + pltpu.roll(bm, m - 1, 0) + pterm
        return (hid * jax.nn.sigmoid(hid)).astype(BF16)

    hk = hidden(uk_ref, w1k_ref, pek_ref)
    hv = hidden(uv_ref, w1v_ref, pev_ref)
    kc_all = _dot(hk, w2k_ref[...]).astype(BF16)
    for j in range(nb):
        for g in range(N_KV):
            r0 = (j * N_KV + g) * nchunk
            kcn_ref[j, g] = kc_all[r0:r0 + nchunk]
            vct_ref[j, g] = _dot_nt(w2vt_ref[...], hv[r0:r0 + nchunk]).astype(BF16)


def _cmp_call(uk, uv, w1k, w1v, pek, pev, w2k, w2vt, bsz, nchunk, nb):
    m = nb * N_KV * nchunk
    feat = uk.shape[1]
    const = lambda *shape: pl.BlockSpec(shape, lambda j: (0,) * len(shape))
    return pl.pallas_call(
        functools.partial(_cmp_kernel, nb=nb, nchunk=nchunk),
        grid=(bsz // nb,),
        in_specs=[pl.BlockSpec((m, feat), lambda j: (j, 0)),
                  pl.BlockSpec((m, feat), lambda j: (j, 0)),
                  const(2 * feat, CMP_HIDDEN),
                  const(2 * feat, CMP_HIDDEN),
                  const(8, 2 * feat),
                  const(8, 2 * feat),
                  const(CMP_HIDDEN, LANE),
                  const(HEAD_DIM, CMP_HIDDEN)],
        out_specs=[pl.BlockSpec((nb, N_KV, nchunk, LANE), lambda j: (j, 0, 0, 0)),
                   pl.BlockSpec((nb, N_KV, HEAD_DIM, nchunk), lambda j: (j, 0, 0, 0))],
        out_shape=[jax.ShapeDtypeStruct((bsz, N_KV, nchunk, LANE), BF16),
                   jax.ShapeDtypeStruct((bsz, N_KV, HEAD_DIM, nchunk), BF16)],
        compiler_params=_cparams(1),
        name="cmp_mlp",
    )(uk, uv, w1k, w1v, pek, pev, w2k, w2vt)


def _attn_kernel(qt_ref, kcn_ref, vct_ref, ks_ref, kw_ref, vst_ref, vwt_ref, gt_ref, ovt_ref, en_ref,
                 o_ref, m_s, l_s, acc_s, ocmp_s, osel_s, *, tq, ns, n_sel):
    i = pl.program_id(2)
    cols = HPG * tq
    qt = jnp.concatenate([qt_ref[0, h] for h in range(HPG)], axis=1)
    q_plain = jnp.concatenate([qt, jnp.zeros_like(qt)], axis=0)
    q_in_tile = lax.broadcasted_iota(jnp.int32, (1, cols), 1) & (tq - 1)
    t_col = i * tq + q_in_tile

    nct = kcn_ref.shape[2]
    s = _dot(kcn_ref[0, 0], q_plain)
    crow = lax.broadcasted_iota(jnp.int32, (nct, cols), 0)
    cmask = (crow * CMP_D + (CMP_L - 1)) <= t_col
    s = jnp.where(cmask, s, NEG)
    mx = jnp.max(s, axis=0, keepdims=True)
    e = jnp.where(cmask, jnp.exp(s - mx), 0.0)
    den = jnp.sum(e, axis=0, keepdims=True)
    p = e / jnp.where(den > 0.0, den, 1.0)
    ocmp_s[...] = _dot(vct_ref[0, 0], p.astype(BF16))

    psum = p[:, 0:tq]
    for h in range(1, HPG):
        psum = psum + p[:, h * tq:(h + 1) * tq]
    p_hi = psum.astype(BF16)
    p_lo = (psum - p_hi.astype(F32)).astype(BF16)
    imp = _dot(ovt_ref[...], p_hi) + _dot(ovt_ref[...], p_lo)
    jrow = lax.broadcasted_iota(jnp.int32, (HEAD_DIM, tq), 0)
    jrow_f = jrow.astype(F32)
    bt = (i * tq + lax.broadcasted_iota(jnp.int32, (HEAD_DIM, tq), 1)) // SEL_L
    in_range = jrow < ns
    forced = (jrow == 0) | (jrow == bt) | (jrow == bt - 1)
    val = jnp.where(jrow <= bt, imp + jnp.where(forced, FORCE_BONUS, 0.0), -1.0)
    val = jnp.where(in_range, val, -3.0)
    sel = jnp.zeros((HEAD_DIM, tq), F32)
    for _ in range(n_sel):
        top = jnp.max(val, axis=0, keepdims=True)
        first = jnp.min(jnp.where(val == top, jrow_f, 1e3), axis=0, keepdims=True)
        hit = jrow_f == first
        sel = jnp.where(hit, 1.0, sel)
        val = jnp.where(hit, -5.0, val)
    bias = jnp.where(in_range & (sel == 0.0), SEL_BIAS, 0.0).astype(BF16)
    q_sel = jnp.concatenate([qt, jnp.concatenate([bias] * HPG, axis=1)], axis=0)

    krow = lax.broadcasted_iota(jnp.int32, (tq, cols), 0)
    causal = krow <= q_in_tile
    band = krow > q_in_tile

    def reset():
        m_s[...] = jnp.full_like(m_s, NEG)
        l_s[...] = jnp.zeros_like(l_s)
        acc_s[...] = jnp.zeros_like(acc_s)

    def update(sc, vt_tile):
        m_old = m_s[...]
        m_new = jnp.maximum(m_old, jnp.max(sc, axis=0, keepdims=True))
        alpha = jnp.exp(m_old - m_new)
        pt = jnp.exp(sc - m_new)
        l_s[...] = alpha * l_s[...] + jnp.sum(pt, axis=0, keepdims=True)
        acc_s[...] = alpha * acc_s[...] + _dot(vt_tile, pt.astype(BF16))
        m_s[...] = m_new

    lane_k = lax.broadcasted_iota(jnp.int32, (tq, LANE), 1)

    def sel_tile(kt, mask):
        k0 = pl.multiple_of(kt * tq, tq)
        kaug = jnp.where(lane_k < HEAD_DIM, ks_ref[0, 0, pl.ds(k0, tq), :], en_ref[pl.ds(k0, tq), :])
        sc = _dot(kaug, q_sel)
        if mask is not None:
            sc = jnp.where(mask, sc, NEG)
        update(sc, vst_ref[0, 0, :, pl.ds(k0, tq)])

    reset()

    def sel_body(kt, carry):
        sel_tile(kt, None)
        return carry

    lax.fori_loop(0, i, sel_body, 0)
    sel_tile(i, causal)
    osel_s[...] = acc_s[...] / l_s[...]

    def win_tile(kt, mask):
        k0 = pl.multiple_of(kt * tq, tq)
        sc = _dot(kw_ref[0, 0, pl.ds(k0, tq), :], q_plain)
        if mask is not None:
            sc = jnp.where(mask, sc, NEG)
        update(sc, vwt_ref[0, 0, :, pl.ds(k0, tq)])

    reset()
    n_back = WINDOW // tq
    for back in range(n_back, 0, -1):
        @pl.when(i >= back)
        def _(back=back):
            win_tile(i - back, band if back == n_back else None)
    win_tile(i, causal)
    o_win = acc_s[...] / l_s[...]

    gt = gt_ref[0, 0]
    o_cmp = ocmp_s[...]
    o_sel = osel_s[...]
    for h in range(HPG):
        c0, c1 = h * tq, (h + 1) * tq
        o_h = (gt[h:h + 1] * o_cmp[:, c0:c1] + gt[SUBLANE + h:SUBLANE + h + 1] * o_sel[:, c0:c1]
               + gt[2 * SUBLANE + h:2 * SUBLANE + h + 1] * o_win[:, c0:c1])
        o_ref[0, h * HEAD_DIM:(h + 1) * HEAD_DIM, :] = o_h.astype(BF16)


def _attn_call(qt, kcn, vct, ksw, vt, gt, ovt, en, tq):
    bsz, _, _, s = qt.shape
    ns = s // SEL_L
    cols = HPG * tq
    nct = kcn.shape[2]
    return pl.pallas_call(
        functools.partial(_attn_kernel, tq=tq, ns=ns, n_sel=min(N_SEL, ns)),
        grid=(bsz, N_KV, s // tq),
        in_specs=[pl.BlockSpec((1, HPG, HEAD_DIM, tq), lambda b, g, i: (b, g, 0, i)),
                  pl.BlockSpec((1, 1, nct, LANE), lambda b, g, i: (b, g, 0, 0)),
                  pl.BlockSpec((1, 1, HEAD_DIM, nct), lambda b, g, i: (b, g, 0, 0)),
                  pl.BlockSpec((1, 1, s, LANE), lambda b, g, i: (b, g, 0, 0)),
                  pl.BlockSpec((1, 1, s, LANE), lambda b, g, i: (b, N_KV + g, 0, 0)),
                  pl.BlockSpec((1, 1, HEAD_DIM, s), lambda b, g, i: (b, g, 0, 0)),
                  pl.BlockSpec((1, 1, HEAD_DIM, s), lambda b, g, i: (b, N_KV + g, 0, 0)),
                  pl.BlockSpec((1, 1, GATE_ROWS, tq), lambda b, g, i: (b, g, 0, i)),
                  pl.BlockSpec((HEAD_DIM, nct), lambda b, g, i: (0, 0)),
                  pl.BlockSpec((s, LANE), lambda b, g, i: (0, 0))],
        out_specs=pl.BlockSpec((1, HPG * HEAD_DIM, tq), lambda b, g, i: (b, g, i)),
        out_shape=jax.ShapeDtypeStruct((bsz, ATTN_W, s), BF16),
        scratch_shapes=[pltpu.VMEM((1, cols), F32), pltpu.VMEM((1, cols), F32),
                        pltpu.VMEM((HEAD_DIM, cols), F32), pltpu.VMEM((HEAD_DIM, cols), F32),
                        pltpu.VMEM((HEAD_DIM, cols), F32)],
        compiler_params=_cparams(3),
        name="nsa_attn",
    )(qt, kcn, vct, ksw, ksw, vt, vt, gt, ovt, en)


def _merge_kernel(x_ref, mod_ref, g_ref, uconv_ref, ot_ref, wmg_ref, bmg_ref, wuc_ref, wua_ref, wo_ref, out_ref):
    x = x_ref[0]
    mod = mod_ref[0]
    d = x.shape[-1]
    hb = _modulated_norm(x, g_ref[...], mod[0:1], mod[1:2]).astype(BF16)
    y_conv = _dot(uconv_ref[0], wuc_ref[...])
    y_attn = _dot_tn(ot_ref[0], wua_ref[...])
    g_conv = jax.nn.sigmoid(_dot(hb, wmg_ref[:, 0:d]) + bmg_ref[:, 0:d])
    g_attn = jax.nn.sigmoid(_dot(hb, wmg_ref[:, d:2 * d]) + bmg_ref[:, d:2 * d])
    mixed = _dot((g_conv * y_conv + g_attn * y_attn).astype(BF16), wo_ref[...])
    out_ref[0] = x + mod[2:3] * mixed


def _merge_call(x, mod, g, uconv, ot, w_mg, b_mg, w_uc, w_ua, w_o, tm):
    bsz, s, d = x.shape
    const = lambda *shape: pl.BlockSpec(shape, lambda b, i: (0,) * len(shape))
    return pl.pallas_call(
        _merge_kernel,
        grid=(bsz, s // tm),
        in_specs=[pl.BlockSpec((1, tm, d), lambda b, i: (b, i, 0)),
                  pl.BlockSpec((1, 6, d), lambda b, i: (b, 0, 0)),
                  const(1, d),
                  pl.BlockSpec((1, tm, CONV_W), lambda b, i: (b, i, 0)),
                  pl.BlockSpec((1, ATTN_W, tm), lambda b, i: (b, 0, i)),
                  const(d, 2 * d), const(1, 2 * d), const(CONV_W, d), const(ATTN_W, d), const(d, d)],
        out_specs=pl.BlockSpec((1, tm, d), lambda b, i: (b, i, 0)),
        out_shape=jax.ShapeDtypeStruct((bsz, s, d), F32),
        compiler_params=_cparams(2),
        name="merge_out",
    )(x, mod, g, uconv, ot, w_mg, b_mg, w_uc, w_ua, w_o)


def _mlp_kernel(x_ref, mod_ref, g_ref, w1_ref, w2_ref, out_ref, *, tf):
    x = x_ref[0]
    mod = mod_ref[0]
    hb = _modulated_norm(x, g_ref[...], mod[3:4], mod[4:5]).astype(BF16)
    acc = None
    for c in range(w1_ref.shape[1] // tf):
        a = jnp.maximum(_dot(hb, w1_ref[:, c * tf:(c + 1) * tf]), 0.0)
        part = _dot((a * a).astype(BF16), w2_ref[c * tf:(c + 1) * tf, :])
        acc = part if acc is None else acc + part
    out_ref[0] = x + mod[5:6] * acc


def _mlp_call(x, mod, g, w1, w2, tm, tf):
    bsz, s, d = x.shape
    const = lambda *shape: pl.BlockSpec(shape, lambda b, i: (0,) * len(shape))
    return pl.pallas_call(
        functools.partial(_mlp_kernel, tf=tf),
        grid=(bsz, s // tm),
        in_specs=[pl.BlockSpec((1, tm, d), lambda b, i: (b, i, 0)),
                  pl.BlockSpec((1, 6, d), lambda b, i: (b, 0, 0)),
                  const(1, d), const(d, w1.shape[1]), const(w1.shape[1], d)],
        out_specs=pl.BlockSpec((1, tm, d), lambda b, i: (b, i, 0)),
        out_shape=jax.ShapeDtypeStruct((bsz, s, d), F32),
        compiler_params=_cparams(2),
        name="relu2_mlp",
    )(x, mod, g, w1, w2)


def _final_kernel(x_ref, g_ref, out_ref):
    x = x_ref[0]
    ms = jnp.mean(x * x, axis=-1, keepdims=True)
    out_ref[0] = x * lax.rsqrt(ms + EPS) * g_ref[...]


def _final_call(x, g, tm):
    bsz, s, d = x.shape
    return pl.pallas_call(
        _final_kernel,
        grid=(bsz, s // tm),
        in_specs=[pl.BlockSpec((1, tm, d), lambda b, i: (b, i, 0)),
                  pl.BlockSpec((1, d), lambda b, i: (0, 0))],
        out_specs=pl.BlockSpec((1, tm, d), lambda b, i: (b, i, 0)),
        out_shape=jax.ShapeDtypeStruct((bsz, s, d), F32),
        compiler_params=_cparams(2),
        name="final_norm",
    )(x, g)


def _pack_weights(w_in, b_in, cmp_k_w2, cmp_v_w2):
    offs = np.concatenate([[0], np.cumsum(IN_SIZES)])
    col = lambda a, k: a[..., int(offs[k]):int(offs[k + 1])]

    def pad_groups(a):
        lead = a.shape[:-1]
        a = a.reshape(lead + (N_KV, HEAD_DIM))
        a = jnp.pad(a, [(0, 0)] * len(lead) + [(0, 0), (0, LANE - HEAD_DIM)])
        return a.reshape(lead + (N_KV * LANE,))

    def gate_slots(a):
        lead = a.shape[:-1]
        a = a.reshape(lead + (3, N_KV, HPG))
        a = jnp.moveaxis(a, -2, -3)
        a = jnp.pad(a, [(0, 0)] * (len(lead) + 2) + [(0, SUBLANE - HPG)])
        return a.reshape(lead + (N_KV * GATE_ROWS,))

    def main(a):
        return jnp.concatenate([col(a, 0), col(a, 1), col(a, 2), col(a, 4), col(a, 5),
                                pad_groups(col(a, 6)), pad_groups(col(a, 8))], axis=-1)

    def trans(a):
        return jnp.concatenate([col(a, 3), col(a, 7), col(a, 9), gate_slots(col(a, 10))], axis=-1)

    w_main = main(w_in).astype(BF16)
    b_main = main(b_in)[:, None, :]
    w_t = jnp.swapaxes(trans(w_in), 1, 2).astype(BF16)
    b_t = trans(b_in)[:, :, None]
    w_mg = col(w_in, 11).astype(BF16)
    b_mg = col(b_in, 11)[:, None, :]
    w2k = jnp.pad(cmp_k_w2, [(0, 0), (0, 0), (0, LANE - HEAD_DIM)]).astype(BF16)
    w2vt = jnp.swapaxes(cmp_v_w2, 1, 2).astype(BF16)
    return w_main, b_main, w_t, b_t, w_mg, b_mg, w2k, w2vt


def _rope_tables(positions):
    half = HEAD_DIM // 2
    inv_freq = ROPE_THETA ** (-jnp.arange(half, dtype=F32) * 2.0 / HEAD_DIM)
    ang = positions.astype(F32)[..., None] * inv_freq
    cos, sin = jnp.cos(ang), jnp.sin(ang)
    zero = jnp.zeros_like(sin)
    reps = LANE // HEAD_DIM
    lane_tab = jnp.stack([jnp.tile(cos, (1, 1, 2 * reps)),
                          jnp.tile(jnp.concatenate([-sin, zero], axis=-1), (1, 1, reps)),
                          jnp.tile(jnp.concatenate([zero, sin], axis=-1), (1, 1, reps))], axis=1)
    t_tab = jnp.stack([jnp.swapaxes(cos, 1, 2), jnp.swapaxes(sin, 1, 2)], axis=1)
    return lane_tab, t_tab


def _static_tables(s, nchunk):
    ns = s // SEL_L
    cstart = np.arange(nchunk) * CMP_D
    jj = np.arange(ns)
    overlap = (cstart[:, None] < (jj[None, :] + 1) * SEL_L) & (cstart[:, None] + CMP_L > jj[None, :] * SEL_L)
    ovt = np.zeros((HEAD_DIM, nchunk), np.float32)
    ovt[:ns] = overlap.T
    en = np.zeros((s, LANE), np.float32)
    en[np.arange(s), HEAD_DIM + np.arange(s) // SEL_L] = 1.0
    return jnp.asarray(ovt, BF16), jnp.asarray(en, BF16)


def kernel(x, c, positions, w_ada, b_ada, g_mix, w_in, b_in, conv_w, cmp_k_w1, cmp_k_w2, cmp_k_pe, cmp_v_w1, cmp_v_w2, cmp_v_pe, w_up_conv, w_up_attn, w_o, g_mlp, w_ff1, w_ff2, g_final):
    bsz, s, d = x.shape
    depth = w_ada.shape[0]
    nchunk = s // CMP_D
    tm = 512
    tq = 256
    nb = 4 if bsz % 4 == 0 else 1
    assert s % tm == 0 and s % tq == 0 and WINDOW % tq == 0 and nchunk == LANE and s // SEL_L <= HEAD_DIM

    w_main, b_main, w_t, b_t, w_mg, b_mg, w2k, w2vt = _pack_weights(w_in, b_in, cmp_k_w2, cmp_v_w2)
    w1k, w1v = cmp_k_w1.astype(BF16), cmp_v_w1.astype(BF16)
    pek = jnp.broadcast_to(cmp_k_pe.reshape(depth, 1, CMP_L * HEAD_DIM), (depth, 8, CMP_L * HEAD_DIM))
    pev = jnp.broadcast_to(cmp_v_pe.reshape(depth, 1, CMP_L * HEAD_DIM), (depth, 8, CMP_L * HEAD_DIM))
    w_uc, w_ua, w_ob = w_up_conv.astype(BF16), w_up_attn.astype(BF16), w_o.astype(BF16)
    w_f1, w_f2 = w_ff1.astype(BF16), w_ff2.astype(BF16)
    rope_tab, rope_tab_t = _rope_tables(positions)
    ovt, en = _static_tables(s, nchunk)

    cond = _ada_call(c, w_ada, b_ada).reshape(depth, bsz, 6, d)

    def chunks(a):
        a = a.reshape(bsz, nchunk, CMP_D, N_KV, HEAD_DIM)
        return jnp.transpose(a, (0, 3, 1, 2, 4)).reshape(bsz * N_KV * nchunk, CMP_D * HEAD_DIM)

    for l in range(depth):
        mod = cond[l]
        uconv, kcvc, ksw, qt, vt, gt = _in_call(x, mod, g_mix[l][None], w_main[l], b_main[l], w_t[l], b_t[l],
                                                rope_tab, rope_tab_t, conv_w[l], tm)
        kcn, vct = _cmp_call(chunks(kcvc[..., :KV_W]), chunks(kcvc[..., KV_W:]), w1k[l], w1v[l], pek[l], pev[l],
                             w2k[l], w2vt[l], bsz, nchunk, nb)
        ot = _attn_call(qt, kcn, vct, ksw, vt, gt, ovt, en, tq)
        x = _merge_call(x, mod, g_mix[l][None], uconv, ot, w_mg[l], b_mg[l], w_uc[l], w_ua[l], w_ob[l], tm)
        x = _mlp_call(x, mod, g_mlp[l][None], w_f1[l], w_f2[l], tm, 1024)
    return _final_call(x, g_final[None], tm)
```

```python
import functools

import jax
import jax.numpy as jnp
import numpy as np
from jax import lax
from jax.experimental import pallas as pl
from jax.experimental.pallas import tpu as pltpu

D_MODEL = 1024
CONV_W = 512
CONV_K = 3
N_HEADS = 8
HEAD_DIM = 64
N_KV = 2
HPG = N_HEADS // N_KV
ATTN_W = N_HEADS * HEAD_DIM
KV_W = N_KV * HEAD_DIM
CMP_L = 32
CMP_D = 16
CMP_HIDDEN = 4 * HEAD_DIM
SEL_L = 64
N_SEL = 8
WINDOW = 512
FORCE_BONUS = 1e4
NEG = -1e30
D_FF = 4 * D_MODEL
ROPE_THETA = 10000.0
EPS = 1e-6
IN_SIZES = (CONV_W, CONV_W, CONV_W, ATTN_W, KV_W, KV_W, KV_W, KV_W, KV_W, KV_W, 3 * N_HEADS, 2 * D_MODEL)

LANE = 128
SUBLANE = 8
SEL_BIAS = -1e9
LOG2_E = 1.4426950408889634
VMEM_LIMIT = 56 * 1024 * 1024
GATE_ROWS = 3 * SUBLANE
V_ROWS = HEAD_DIM + 2 * SUBLANE

_C_CB, _C_CC, _C_CU = 0, CONV_W, 2 * CONV_W
_C_KC = 3 * CONV_W
_C_VC = _C_KC + KV_W
_C_KS = _C_VC + KV_W
_C_KW = _C_KS + N_KV * LANE
N_MAIN = _C_KW + N_KV * LANE
_R_Q = 0
_R_VS = ATTN_W
_R_VW = _R_VS + KV_W
_R_G = _R_VW + KV_W
N_TRANS = _R_G + N_KV * GATE_ROWS

F32 = jnp.float32
BF16 = jnp.bfloat16


def _cparams(n_grid):
    return pltpu.CompilerParams(dimension_semantics=("arbitrary",) * n_grid,
                                vmem_limit_bytes=VMEM_LIMIT)


def _dot(a, b):
    return jnp.dot(a, b, preferred_element_type=F32)


def _dot_nt(a, b):
    return lax.dot_general(a, b, (((1,), (1,)), ((), ())), preferred_element_type=F32)


def _dot_tn(a, b):
    return lax.dot_general(a, b, (((0,), (0,)), ((), ())), preferred_element_type=F32)


def _modulated_norm(x, g, shift, scale):
    ms = jnp.mean(x * x, axis=-1, keepdims=True)
    y = x * lax.rsqrt(ms + EPS) * g
    return y * (1.0 + scale) + shift


def _ada_kernel(c_ref, w_ref, b_ref, o_ref):
    c = c_ref[...]
    ca = (c * jax.nn.sigmoid(c)).astype(BF16)
    o_ref[0] = _dot(ca, w_ref[0].astype(BF16)) + b_ref[0]


def _ada_call(c, w_ada, b_ada):
    depth, d, n = w_ada.shape
    bsz = c.shape[0]
    tn = 512
    return pl.pallas_call(
        _ada_kernel,
        grid=(depth, n // tn),
        in_specs=[pl.BlockSpec((bsz, d), lambda l, j: (0, 0)),
                  pl.BlockSpec((1, d, tn), lambda l, j: (l, 0, j)),
                  pl.BlockSpec((1, 1, tn), lambda l, j: (l, 0, j))],
        out_specs=pl.BlockSpec((1, bsz, tn), lambda l, j: (l, 0, j)),
        out_shape=jax.ShapeDtypeStruct((depth, bsz, n), F32),
        compiler_params=_cparams(2),
        name="ada_cond",
    )(c, w_ada, b_ada.reshape(depth, 1, n))


def _in_kernel(x_ref, mod_ref, g_ref, w_ref, b_ref, wt_ref, bt_ref, rope_ref, ropet_ref, cw_ref,
               uconv_ref, ukv_ref, ksw_ref, qt_ref, vt_ref, gt_ref, carry_ref, stagek_ref, stagev_ref):
    i = pl.program_id(1)
    x = x_ref[0]
    tm = x.shape[0]
    mod = mod_ref[0]
    hb = _modulated_norm(x, g_ref[...], mod[0:1], mod[1:2]).astype(BF16)

    def seg(a, b):
        return _dot(hb, w_ref[:, a:b]) + b_ref[:, a:b]

    @pl.when(i == 0)
    def _():
        carry_ref[...] = jnp.zeros_like(carry_ref)

    v = seg(_C_CC, _C_CU) * seg(_C_CU, _C_KC)
    prev = carry_ref[...]
    row = lax.broadcasted_iota(jnp.int32, v.shape, 0)
    p1 = jnp.broadcast_to(prev[7:8], v.shape)
    p2 = jnp.broadcast_to(prev[6:7], v.shape)
    v1 = jnp.where(row == 0, p1, pltpu.roll(v, 1, 0))
    v2 = jnp.where(row == 0, p2, jnp.where(row == 1, p1, pltpu.roll(v, 2, 0)))
    cw = cw_ref[...]
    yc = cw[0:1] * v2 + cw[1:2] * v1 + cw[2:3] * v
    carry_ref[...] = v[tm - 8:tm]
    uconv_ref[0] = (seg(_C_CB, _C_CC) * yc).astype(BF16)

    cos2, sa2, sb2 = rope_ref[0, 0], rope_ref[0, 1], rope_ref[0, 2]
    lane = lax.broadcasted_iota(jnp.int32, cos2.shape, 1)
    low = lane < HEAD_DIM
    cos1, sa1, sb1 = jnp.where(low, cos2, 0.0), jnp.where(low, sa2, 0.0), jnp.where(low, sb2, 0.0)

    def rope(z, c, sa, sb):
        return z * c + pltpu.roll(z, LANE - HEAD_DIM // 2, 1) * sa + pltpu.roll(z, HEAD_DIM // 2, 1) * sb

    stagek_ref[...] = rope(seg(_C_KC, _C_VC), cos2, sa2, sb2)
    stagev_ref[...] = seg(_C_VC, _C_KS)
    n_ch = tm // CMP_D
    low_ch = lax.broadcasted_iota(jnp.int32, (n_ch, LANE), 1) < HEAD_DIM
    for a, stage_ref in enumerate((stagek_ref, stagev_ref)):
        for mm in range(CMP_D // 2):
            p0 = stage_ref[pl.ds(2 * mm, n_ch, stride=CMP_D), :]
            p1 = stage_ref[pl.ds(2 * mm + 1, n_ch, stride=CMP_D), :]
            lanes = slice(mm * LANE, (mm + 1) * LANE)
            ukv_ref[0, a, 0, :, lanes] = jnp.where(low_ch, p0, pltpu.roll(p1, HEAD_DIM, 1)).astype(BF16)
            ukv_ref[0, a, 1, :, lanes] = jnp.where(low_ch, pltpu.roll(p0, HEAD_DIM, 1), p1).astype(BF16)
    for r in range(2 * N_KV):
        zk = seg(_C_KS + r * LANE, _C_KS + (r + 1) * LANE)
        ksw_ref[0, r] = rope(zk, cos1, sa1, sb1).astype(BF16)

    zt = _dot_nt(wt_ref[...], hb) + bt_ref[...]
    ct, st = ropet_ref[0, 0], ropet_ref[0, 1]
    half = HEAD_DIM // 2
    scale = HEAD_DIM ** -0.5 * LOG2_E
    for h in range(N_HEADS):
        x1 = zt[_R_Q + h * HEAD_DIM:_R_Q + h * HEAD_DIM + half]
        x2 = zt[_R_Q + h * HEAD_DIM + half:_R_Q + (h + 1) * HEAD_DIM]
        qt_ref[0, h, 0:half, :] = ((x1 * ct - x2 * st) * scale).astype(BF16)
        qt_ref[0, h, half:HEAD_DIM, :] = ((x2 * ct + x1 * st) * scale).astype(BF16)
    for r in range(2 * N_KV):
        vt_ref[0, r, 0:HEAD_DIM, :] = zt[_R_VS + r * HEAD_DIM:_R_VS + (r + 1) * HEAD_DIM].astype(BF16)
        vt_ref[0, r, HEAD_DIM:V_ROWS, :] = jnp.ones((V_ROWS - HEAD_DIM, tm), BF16)
    for gi in range(N_KV):
        gt_ref[0, gi] = jax.nn.sigmoid(zt[_R_G + gi * GATE_ROWS:_R_G + (gi + 1) * GATE_ROWS])


def _in_call(x, mod, g, w_main, b_main, w_t, b_t, rope_tab, rope_tab_t, conv_w, tm):
    bsz, s, d = x.shape
    const = lambda *shape: pl.BlockSpec(shape, lambda b, i: (0,) * len(shape))
    return pl.pallas_call(
        _in_kernel,
        grid=(bsz, s // tm),
        in_specs=[pl.BlockSpec((1, tm, d), lambda b, i: (b, i, 0)),
                  pl.BlockSpec((1, 6, d), lambda b, i: (b, 0, 0)),
                  const(1, d),
                  const(d, N_MAIN),
                  const(1, N_MAIN),
                  const(N_TRANS, d),
                  const(N_TRANS, 1),
                  pl.BlockSpec((1, 3, tm, LANE), lambda b, i: (b, 0, i, 0)),
                  pl.BlockSpec((1, 2, HEAD_DIM // 2, tm), lambda b, i: (b, 0, 0, i)),
                  const(CONV_K, CONV_W)],
        out_specs=[pl.BlockSpec((1, tm, CONV_W), lambda b, i: (b, i, 0)),
                   pl.BlockSpec((1, 2, N_KV, tm // CMP_D, CMP_D * HEAD_DIM), lambda b, i: (b, 0, 0, i, 0)),
                   pl.BlockSpec((1, 2 * N_KV, tm, LANE), lambda b, i: (b, 0, i, 0)),
                   pl.BlockSpec((1, N_HEADS, HEAD_DIM, tm), lambda b, i: (b, 0, 0, i)),
                   pl.BlockSpec((1, 2 * N_KV, V_ROWS, tm), lambda b, i: (b, 0, 0, i)),
                   pl.BlockSpec((1, N_KV, GATE_ROWS, tm), lambda b, i: (b, 0, 0, i))],
        out_shape=[jax.ShapeDtypeStruct((bsz, s, CONV_W), BF16),
                   jax.ShapeDtypeStruct((bsz, 2, N_KV, s // CMP_D, CMP_D * HEAD_DIM), BF16),
                   jax.ShapeDtypeStruct((bsz, 2 * N_KV, s, LANE), BF16),
                   jax.ShapeDtypeStruct((bsz, N_HEADS, HEAD_DIM, s), BF16),
                   jax.ShapeDtypeStruct((bsz, 2 * N_KV, V_ROWS, s), BF16),
                   jax.ShapeDtypeStruct((bsz, N_KV, GATE_ROWS, s), F32)],
        scratch_shapes=[pltpu.VMEM((8, CONV_W), F32), pltpu.VMEM((tm, LANE), F32), pltpu.VMEM((tm, LANE), F32)],
        compiler_params=_cparams(2),
        name="in_proj",
    )(x, mod, g, w_main, b_main, w_t, b_t, rope_tab, rope_tab_t, conv_w)


def _cmp_kernel(uk_ref, uv_ref, w1k_ref, w1v_ref, pek_ref, pev_ref, w2k_ref, w2vt_ref,
                kcn_ref, vct_ref, *, nb, nchunk):
    m = nb * N_KV * nchunk
    kh = w1k_ref.shape[0] // 2

    def hidden(u_ref, w1_ref, pe_ref):
        u = jnp.concatenate([u_ref[j, 0, g] for j in range(nb) for g in range(N_KV)], axis=0)
        pe = pe_ref[...]
        pe_hi = pe.astype(BF16)
        pe_lo = (pe - pe_hi.astype(F32)).astype(BF16)
        w1 = w1_ref[...]
        pterm = (_dot(pe_hi, w1) + _dot(pe_lo, w1))[0:1]
        a = _dot(u, w1[0:kh])
        bm = _dot(u, w1[kh:2 * kh])
        hid = a + pltpu.roll(bm, m - 1, 0) + pterm
        return (hid * jax.nn.sigmoid(hid)).astype(BF16)

    hk = hidden(uk_ref, w1k_ref, pek_ref)
    hv = hidden(uv_ref, w1v_ref, pev_ref)
    kc_all = _dot(hk, w2k_ref[...]).astype(BF16)
    for j in range(nb):
        for g in range(N_KV):
            r0 = (j * N_KV + g) * nchunk
            kcn_ref[j, g] = kc_all[r0:r0 + nchunk]
            vct_ref[j, g] = _dot_nt(w2vt_ref[...], hv[r0:r0 + nchunk]).astype(BF16)


def _cmp_call(ukv, w1k, w1v, pek, pev, w2k, w2vt, nb):
    bsz, _, _, nchunk, feat = ukv.shape
    const = lambda *shape: pl.BlockSpec(shape, lambda j: (0,) * len(shape))
    return pl.pallas_call(
        functools.partial(_cmp_kernel, nb=nb, nchunk=nchunk),
        grid=(bsz // nb,),
        in_specs=[pl.BlockSpec((nb, 1, N_KV, nchunk, feat), lambda j: (j, 0, 0, 0, 0)),
                  pl.BlockSpec((nb, 1, N_KV, nchunk, feat), lambda j: (j, 1, 0, 0, 0)),
                  const(2 * feat, CMP_HIDDEN),
                  const(2 * feat, CMP_HIDDEN),
                  const(8, 2 * feat),
                  const(8, 2 * feat),
                  const(CMP_HIDDEN, LANE),
                  const(HEAD_DIM, CMP_HIDDEN)],
        out_specs=[pl.BlockSpec((nb, N_KV, nchunk, LANE), lambda j: (j, 0, 0, 0)),
                   pl.BlockSpec((nb, N_KV, HEAD_DIM, nchunk), lambda j: (j, 0, 0, 0))],
        out_shape=[jax.ShapeDtypeStruct((bsz, N_KV, nchunk, LANE), BF16),
                   jax.ShapeDtypeStruct((bsz, N_KV, HEAD_DIM, nchunk), BF16)],
        compiler_params=_cparams(1),
        name="cmp_mlp",
    )(ukv, ukv, w1k, w1v, pek, pev, w2k, w2vt)


def _attn_kernel(qt_ref, kcn_ref, vct_ref, ks_ref, kw_ref, vst_ref, vwt_ref, gt_ref, ovt_ref, en_ref,
                 o_ref, m_s, acc_s, mw_s, accw_s, ocmp_s,
                 sa_s, sb_s, wa_s, wb_s, mxa_s, mxb_s, mxwa_s, mxwb_s, *, tq, ns, n_sel):
    i = pl.program_id(2)
    cols = HPG * tq
    qt = jnp.concatenate([qt_ref[0, h] for h in range(HPG)], axis=1)
    q_plain = jnp.concatenate([qt, jnp.zeros_like(qt)], axis=0)
    q_in_tile = lax.broadcasted_iota(jnp.int32, (1, cols), 1) & (tq - 1)
    t_col = i * tq + q_in_tile

    krow = lax.broadcasted_iota(jnp.int32, (tq, cols), 0)
    causal = krow <= q_in_tile
    band = krow > q_in_tile
    lane_k = lax.broadcasted_iota(jnp.int32, (tq, LANE), 1)

    def score(k_tile, q_aug, buf, mask):
        sc_ref, mx_ref = buf
        sc = _dot(k_tile, q_aug)
        if mask is not None:
            sc = jnp.where(mask, sc, NEG)
        sc_ref[...] = sc
        mx_ref[...] = jnp.max(sc, axis=0, keepdims=True)

    def accumulate(buf, vt_tile, m_r, acc_r):
        sc_ref, mx_ref = buf
        m_old = m_r[...]
        m_new = jnp.maximum(m_old, mx_ref[...])
        alpha = jnp.exp2(m_old - m_new)
        pt = jnp.exp2(sc_ref[...] - m_new)
        acc_r[...] = alpha * acc_r[...] + _dot(vt_tile, pt.astype(BF16))
        m_r[...] = m_new

    def normalized(acc_r):
        acc = acc_r[...]
        return acc[0:HEAD_DIM] / acc[HEAD_DIM:HEAD_DIM + 1]

    for m_r, acc_r in ((m_s, acc_s), (mw_s, accw_s)):
        m_r[...] = jnp.full_like(m_r, NEG)
        acc_r[...] = jnp.zeros_like(acc_r)

    n_back = WINDOW // tq
    brow = lax.broadcasted_iota(jnp.int32, (HEAD_DIM, cols), 0)
    wbias = jnp.where((brow >= 1) & (brow <= n_back) & (brow > i), SEL_BIAS, 0.0).astype(BF16)
    q_win = jnp.concatenate([qt, wbias], axis=0)
    wbufs = ((wa_s, mxwa_s), (wb_s, mxwb_s))

    def win_score(back, mask):
        k0 = pl.multiple_of(jnp.maximum(i - back, 0) * tq, tq)
        kaug = jnp.where(lane_k == HEAD_DIM + back, 1.0, kw_ref[0, 0, pl.ds(k0, tq), :])
        score(kaug, q_win, wbufs[back & 1], mask)

    def win_accumulate(back):
        k0 = pl.multiple_of(jnp.maximum(i - back, 0) * tq, tq)
        accumulate(wbufs[back & 1], vwt_ref[0, 0, :, pl.ds(k0, tq)], mw_s, accw_s)

    win_score(0, causal)
    for back in range(1, n_back + 1):
        win_score(back, band if back == n_back else None)
        win_accumulate(back - 1)

    nct = kcn_ref.shape[2]
    s = _dot(kcn_ref[0, 0], q_plain)
    crow = lax.broadcasted_iota(jnp.int32, (nct, cols), 0)
    cmask = (crow * CMP_D + (CMP_L - 1)) <= t_col
    s = jnp.where(cmask, s, NEG)
    mx = jnp.max(s, axis=0, keepdims=True)
    e = jnp.where(cmask, jnp.exp2(s - mx), 0.0)
    den = jnp.sum(e, axis=0, keepdims=True)
    p = e / jnp.where(den > 0.0, den, 1.0)
    ocmp_s[...] = _dot(vct_ref[0, 0], p.astype(BF16))
    win_accumulate(n_back)

    psum = p[:, 0:tq]
    for h in range(1, HPG):
        psum = psum + p[:, h * tq:(h + 1) * tq]
    p_hi = psum.astype(BF16)
    p_lo = (psum - p_hi.astype(F32)).astype(BF16)
    imp = _dot(ovt_ref[...], p_hi) + _dot(ovt_ref[...], p_lo)
    jrow = lax.broadcasted_iota(jnp.int32, (HEAD_DIM, tq), 0)
    jrow_f = jrow.astype(F32)
    bt = (i * tq + lax.broadcasted_iota(jnp.int32, (HEAD_DIM, tq), 1)) // SEL_L
    in_range = jrow < ns
    forced = (jrow == 0) | (jrow == bt) | (jrow == bt - 1)
    val = jnp.where(jrow <= bt, imp + jnp.where(forced, FORCE_BONUS, 0.0), -1.0)
    val = jnp.where(in_range, val, -3.0)
    sel = jnp.zeros((HEAD_DIM, tq), F32)
    for _ in range(n_sel):
        top = jnp.max(val, axis=0, keepdims=True)
        first = jnp.min(jnp.where(val == top, jrow_f, 1e3), axis=0, keepdims=True)
        hit = jrow_f == first
        sel = jnp.where(hit, 1.0, sel)
        val = jnp.where(hit, -5.0, val)
    bias = jnp.where(in_range & (sel == 0.0), SEL_BIAS, 0.0).astype(BF16)
    q_sel = jnp.concatenate([qt, jnp.concatenate([bias] * HPG, axis=1)], axis=0)

    buf_a, buf_b = (sa_s, mxa_s), (sb_s, mxb_s)

    def sel_score(kt, mask, buf):
        k0 = pl.multiple_of(kt * tq, tq)
        kaug = jnp.where(lane_k < HEAD_DIM, ks_ref[0, 0, pl.ds(k0, tq), :], en_ref[pl.ds(k0, tq), :])
        score(kaug, q_sel, buf, mask)

    def sel_accumulate(kt, buf):
        k0 = pl.multiple_of(kt * tq, tq)
        accumulate(buf, vst_ref[0, 0, :, pl.ds(k0, tq)], m_s, acc_s)

    sel_score(0, causal | (i > 0), buf_a)

    def sel_pair(j, carry):
        t = 2 * j
        sel_score(t + 1, None, buf_b)
        sel_accumulate(t, buf_a)
        sel_score(t + 2, None, buf_a)
        sel_accumulate(t + 1, buf_b)
        return carry

    lax.fori_loop(0, jnp.maximum(i - 1, 0) // 2, sel_pair, 0)

    @pl.when(i == 0)
    def _():
        sel_accumulate(0, buf_a)

    @pl.when((i & 1) == 1)
    def _():
        sel_score(i, causal, buf_b)
        sel_accumulate(i - 1, buf_a)
        sel_accumulate(i, buf_b)

    @pl.when(((i & 1) == 0) & (i > 0))
    def _():
        sel_score(i - 1, None, buf_b)
        sel_accumulate(i - 2, buf_a)
        sel_score(i, causal, buf_a)
        sel_accumulate(i - 1, buf_b)
        sel_accumulate(i, buf_a)

    gt = gt_ref[0, 0]
    o_cmp = ocmp_s[...]
    o_sel = normalized(acc_s)
    o_win = normalized(accw_s)
    for h in range(HPG):
        c0, c1 = h * tq, (h + 1) * tq
        o_h = (gt[h:h + 1] * o_cmp[:, c0:c1] + gt[SUBLANE + h:SUBLANE + h + 1] * o_sel[:, c0:c1]
               + gt[2 * SUBLANE + h:2 * SUBLANE + h + 1] * o_win[:, c0:c1])
        o_ref[0, h * HEAD_DIM:(h + 1) * HEAD_DIM, :] = o_h.astype(BF16)


def _attn_call(qt, kcn, vct, ksw, vt, gt, ovt, en, tq):
    bsz, _, _, s = qt.shape
    ns = s // SEL_L
    cols = HPG * tq
    nct = kcn.shape[2]
    return pl.pallas_call(
        functools.partial(_attn_kernel, tq=tq, ns=ns, n_sel=min(N_SEL, ns)),
        grid=(bsz, N_KV, s // tq),
        in_specs=[pl.BlockSpec((1, HPG, HEAD_DIM, tq), lambda b, g, i: (b, g, 0, i)),
                  pl.BlockSpec((1, 1, nct, LANE), lambda b, g, i: (b, g, 0, 0)),
                  pl.BlockSpec((1, 1, HEAD_DIM, nct), lambda b, g, i: (b, g, 0, 0)),
                  pl.BlockSpec((1, 1, s, LANE), lambda b, g, i: (b, g, 0, 0)),
                  pl.BlockSpec((1, 1, s, LANE), lambda b, g, i: (b, N_KV + g, 0, 0)),
                  pl.BlockSpec((1, 1, V_ROWS, s), lambda b, g, i: (b, g, 0, 0)),
                  pl.BlockSpec((1, 1, V_ROWS, s), lambda b, g, i: (b, N_KV + g, 0, 0)),
                  pl.BlockSpec((1, 1, GATE_ROWS, tq), lambda b, g, i: (b, g, 0, i)),
                  pl.BlockSpec((HEAD_DIM, nct), lambda b, g, i: (0, 0)),
                  pl.BlockSpec((s, LANE), lambda b, g, i: (0, 0))],
        out_specs=pl.BlockSpec((1, HPG * HEAD_DIM, tq), lambda b, g, i: (b, g, i)),
        out_shape=jax.ShapeDtypeStruct((bsz, ATTN_W, s), BF16),
        scratch_shapes=[pltpu.VMEM((1, cols), F32), pltpu.VMEM((V_ROWS, cols), F32),
                        pltpu.VMEM((1, cols), F32), pltpu.VMEM((V_ROWS, cols), F32),
                        pltpu.VMEM((HEAD_DIM, cols), F32),
                        pltpu.VMEM((tq, cols), F32), pltpu.VMEM((tq, cols), F32),
                        pltpu.VMEM((tq, cols), F32), pltpu.VMEM((tq, cols), F32),
                        pltpu.VMEM((1, cols), F32), pltpu.VMEM((1, cols), F32),
                        pltpu.VMEM((1, cols), F32), pltpu.VMEM((1, cols), F32)],
        compiler_params=_cparams(3),
        name="nsa_attn",
    )(qt, kcn, vct, ksw, ksw, vt, vt, gt, ovt, en)


def _merge_kernel(x_ref, mod_ref, g_ref, uconv_ref, ot_ref, wmg_ref, bmg_ref, wuc_ref, wua_ref, wo_ref, out_ref):
    x = x_ref[0]
    mod = mod_ref[0]
    d = x.shape[-1]
    hb = _modulated_norm(x, g_ref[...], mod[0:1], mod[1:2]).astype(BF16)
    y_conv = _dot(uconv_ref[0], wuc_ref[...])
    y_attn = _dot_tn(ot_ref[0], wua_ref[...])
    g_conv = jax.nn.sigmoid(_dot(hb, wmg_ref[:, 0:d]) + bmg_ref[:, 0:d])
    g_attn = jax.nn.sigmoid(_dot(hb, wmg_ref[:, d:2 * d]) + bmg_ref[:, d:2 * d])
    mixed = _dot((g_conv * y_conv + g_attn * y_attn).astype(BF16), wo_ref[...])
    out_ref[0] = x + mod[2:3] * mixed


def _merge_call(x, mod, g, uconv, ot, w_mg, b_mg, w_uc, w_ua, w_o, tm):
    bsz, s, d = x.shape
    const = lambda *shape: pl.BlockSpec(shape, lambda b, i: (0,) * len(shape))
    return pl.pallas_call(
        _merge_kernel,
        grid=(bsz, s // tm),
        in_specs=[pl.BlockSpec((1, tm, d), lambda b, i: (b, i, 0)),
                  pl.BlockSpec((1, 6, d), lambda b, i: (b, 0, 0)),
                  const(1, d),
                  pl.BlockSpec((1, tm, CONV_W), lambda b, i: (b, i, 0)),
                  pl.BlockSpec((1, ATTN_W, tm), lambda b, i: (b, 0, i)),
                  const(d, 2 * d), const(1, 2 * d), const(CONV_W, d), const(ATTN_W, d), const(d, d)],
        out_specs=pl.BlockSpec((1, tm, d), lambda b, i: (b, i, 0)),
        out_shape=jax.ShapeDtypeStruct((bsz, s, d), F32),
        compiler_params=_cparams(2),
        name="merge_out",
    )(x, mod, g, uconv, ot, w_mg, b_mg, w_uc, w_ua, w_o)


def _mlp_kernel(x_ref, mod_ref, g_ref, w1_ref, w2_ref, *rest, tf):
    out_ref = rest[-1]
    x = x_ref[0]
    mod = mod_ref[0]
    hb = _modulated_norm(x, g_ref[...], mod[3:4], mod[4:5]).astype(BF16)
    acc = None
    for c in range(w1_ref.shape[1] // tf):
        a = jnp.maximum(_dot(hb, w1_ref[:, c * tf:(c + 1) * tf]), 0.0)
        part = _dot((a * a).astype(BF16), w2_ref[c * tf:(c + 1) * tf, :])
        acc = part if acc is None else acc + part
    y = x + mod[5:6] * acc
    if len(rest) == 2:
        ms = jnp.mean(y * y, axis=-1, keepdims=True)
        y = y * lax.rsqrt(ms + EPS) * rest[0][...]
    out_ref[0] = y


def _mlp_call(x, mod, g, w1, w2, g_final, tm, tf):
    bsz, s, d = x.shape
    const = lambda *shape: pl.BlockSpec(shape, lambda b, i: (0,) * len(shape))
    extra = () if g_final is None else (g_final,)
    return pl.pallas_call(
        functools.partial(_mlp_kernel, tf=tf),
        grid=(bsz, s // tm),
        in_specs=[pl.BlockSpec((1, tm, d), lambda b, i: (b, i, 0)),
                  pl.BlockSpec((1, 6, d), lambda b, i: (b, 0, 0)),
                  const(1, d), const(d, w1.shape[1]), const(w1.shape[1], d)] + [const(1, d)] * len(extra),
        out_specs=pl.BlockSpec((1, tm, d), lambda b, i: (b, i, 0)),
        out_shape=jax.ShapeDtypeStruct((bsz, s, d), F32),
        compiler_params=_cparams(2),
        name="relu2_mlp",
    )(x, mod, g, w1, w2, *extra)


def _pack_weights(w_in, b_in, cmp_k_w2, cmp_v_w2):
    offs = np.concatenate([[0], np.cumsum(IN_SIZES)])
    col = lambda a, k: a[..., int(offs[k]):int(offs[k + 1])]

    def pad_groups(a):
        lead = a.shape[:-1]
        a = a.reshape(lead + (N_KV, HEAD_DIM))
        a = jnp.pad(a, [(0, 0)] * len(lead) + [(0, 0), (0, LANE - HEAD_DIM)])
        return a.reshape(lead + (N_KV * LANE,))

    def gate_slots(a):
        lead = a.shape[:-1]
        a = a.reshape(lead + (3, N_KV, HPG))
        a = jnp.moveaxis(a, -2, -3)
        a = jnp.pad(a, [(0, 0)] * (len(lead) + 2) + [(0, SUBLANE - HPG)])
        return a.reshape(lead + (N_KV * GATE_ROWS,))

    def main(a):
        return jnp.concatenate([col(a, 0), col(a, 1), col(a, 2), col(a, 4), col(a, 5),
                                pad_groups(col(a, 6)), pad_groups(col(a, 8))], axis=-1)

    def trans(a):
        return jnp.concatenate([col(a, 3), col(a, 7), col(a, 9), gate_slots(col(a, 10))], axis=-1)

    w_main = main(w_in).astype(BF16)
    b_main = main(b_in)[:, None, :]
    w_t = jnp.swapaxes(trans(w_in), 1, 2).astype(BF16)
    b_t = trans(b_in)[:, :, None]
    w_mg = col(w_in, 11).astype(BF16)
    b_mg = col(b_in, 11)[:, None, :]
    w2k = jnp.pad(cmp_k_w2, [(0, 0), (0, 0), (0, LANE - HEAD_DIM)]).astype(BF16)
    w2vt = jnp.swapaxes(cmp_v_w2, 1, 2).astype(BF16)
    return w_main, b_main, w_t, b_t, w_mg, b_mg, w2k, w2vt


def _rope_tables(positions):
    half = HEAD_DIM // 2
    inv_freq = ROPE_THETA ** (-jnp.arange(half, dtype=F32) * 2.0 / HEAD_DIM)
    ang = positions.astype(F32)[..., None] * inv_freq
    cos, sin = jnp.cos(ang), jnp.sin(ang)
    zero = jnp.zeros_like(sin)
    reps = LANE // HEAD_DIM
    lane_tab = jnp.stack([jnp.tile(cos, (1, 1, 2 * reps)),
                          jnp.tile(jnp.concatenate([-sin, zero], axis=-1), (1, 1, reps)),
                          jnp.tile(jnp.concatenate([zero, sin], axis=-1), (1, 1, reps))], axis=1)
    t_tab = jnp.stack([jnp.swapaxes(cos, 1, 2), jnp.swapaxes(sin, 1, 2)], axis=1)
    return lane_tab, t_tab


def _static_tables(s, nchunk):
    ns = s // SEL_L
    cstart = np.arange(nchunk) * CMP_D
    jj = np.arange(ns)
    overlap = (cstart[:, None] < (jj[None, :] + 1) * SEL_L) & (cstart[:, None] + CMP_L > jj[None, :] * SEL_L)
    ovt = np.zeros((HEAD_DIM, nchunk), np.float32)
    ovt[:ns] = overlap.T
    en = np.zeros((s, LANE), np.float32)
    en[np.arange(s), HEAD_DIM + np.arange(s) // SEL_L] = 1.0
    return jnp.asarray(ovt, BF16), jnp.asarray(en, BF16)


def kernel(x, c, positions, w_ada, b_ada, g_mix, w_in, b_in, conv_w, cmp_k_w1, cmp_k_w2, cmp_k_pe, cmp_v_w1, cmp_v_w2, cmp_v_pe, w_up_conv, w_up_attn, w_o, g_mlp, w_ff1, w_ff2, g_final):
    bsz, s, d = x.shape
    depth = w_ada.shape[0]
    nchunk = s // CMP_D
    tm = 512
    tq = 256
    nb = 4 if bsz % 4 == 0 else 1
    assert s % tm == 0 and s % tq == 0 and WINDOW % tq == 0 and nchunk == LANE and s // SEL_L <= HEAD_DIM

    w_main, b_main, w_t, b_t, w_mg, b_mg, w2k, w2vt = _pack_weights(w_in, b_in, cmp_k_w2, cmp_v_w2)
    w1k, w1v = cmp_k_w1.astype(BF16), cmp_v_w1.astype(BF16)
    pek = jnp.broadcast_to(cmp_k_pe.reshape(depth, 1, CMP_L * HEAD_DIM), (depth, 8, CMP_L * HEAD_DIM))
    pev = jnp.broadcast_to(cmp_v_pe.reshape(depth, 1, CMP_L * HEAD_DIM), (depth, 8, CMP_L * HEAD_DIM))
    w_uc, w_ua, w_ob = w_up_conv.astype(BF16), w_up_attn.astype(BF16), w_o.astype(BF16)
    w_f1, w_f2 = w_ff1.astype(BF16), w_ff2.astype(BF16)
    rope_tab, rope_tab_t = _rope_tables(positions)
    ovt, en = _static_tables(s, nchunk)

    cond = _ada_call(c, w_ada, b_ada).reshape(depth, bsz, 6, d)

    for l in range(depth):
        mod = cond[l]
        uconv, ukv, ksw, qt, vt, gt = _in_call(x, mod, g_mix[l][None], w_main[l], b_main[l], w_t[l], b_t[l],
                                               rope_tab, rope_tab_t, conv_w[l], tm)
        kcn, vct = _cmp_call(ukv, w1k[l], w1v[l], pek[l], pev[l], w2k[l], w2vt[l], nb)
        ot = _attn_call(qt, kcn, vct, ksw, vt, gt, ovt, en, tq)
        x = _merge_call(x, mod, g_mix[l][None], uconv, ot, w_mg[l], b_mg[l], w_uc[l], w_ua[l], w_ob[l], tm)
        x = _mlp_call(x, mod, g_mlp[l][None], w_f1[l], w_f2[l], g_final[None] if l == depth - 1 else None, tm, 1024)
    return x
```

```python
import functools

import jax
import jax.numpy as jnp
import numpy as np
from jax import lax
from jax.experimental import pallas as pl
from jax.experimental.pallas import tpu as pltpu

D_MODEL = 1024
CONV_W = 512
CONV_K = 3
N_HEADS = 8
HEAD_DIM = 64
N_KV = 2
HPG = N_HEADS // N_KV
ATTN_W = N_HEADS * HEAD_DIM
KV_W = N_KV * HEAD_DIM
CMP_L = 32
CMP_D = 16
CMP_HIDDEN = 4 * HEAD_DIM
SEL_L = 64
N_SEL = 8
WINDOW = 512
FORCE_BONUS = 1e4
NEG = -1e30
D_FF = 4 * D_MODEL
ROPE_THETA = 10000.0
EPS = 1e-6
IN_SIZES = (CONV_W, CONV_W, CONV_W, ATTN_W, KV_W, KV_W, KV_W, KV_W, KV_W, KV_W, 3 * N_HEADS, 2 * D_MODEL)

LANE = 128
SUBLANE = 8
SEL_BIAS = -1e9
LOG2_E = 1.4426950408889634
VMEM_LIMIT = 56 * 1024 * 1024
GATE_ROWS = 3 * SUBLANE
V_ROWS = HEAD_DIM + 2 * SUBLANE

_C_CB, _C_CC, _C_CU = 0, CONV_W, 2 * CONV_W
_C_KC = 3 * CONV_W
_C_VC = _C_KC + KV_W
_C_KS = _C_VC + KV_W
_C_KW = _C_KS + N_KV * LANE
N_MAIN = _C_KW + N_KV * LANE
_R_Q = 0
_R_VS = ATTN_W
_R_VW = _R_VS + KV_W
_R_G = _R_VW + KV_W
N_TRANS = _R_G + N_KV * GATE_ROWS

F32 = jnp.float32
BF16 = jnp.bfloat16


def _cparams(n_grid):
    return pltpu.CompilerParams(dimension_semantics=("arbitrary",) * n_grid,
                                vmem_limit_bytes=VMEM_LIMIT)


def _whole(*shape):
    return pl.BlockSpec(shape, lambda *_: (0,) * len(shape))


def _layer(l, *shape):
    return pl.BlockSpec((None,) + shape, lambda *_: (l,) + (0,) * len(shape))


def _dot(a, b):
    return jnp.dot(a, b, preferred_element_type=F32)


def _dot_nt(a, b):
    return lax.dot_general(a, b, (((1,), (1,)), ((), ())), preferred_element_type=F32)


def _dot_tn(a, b):
    return lax.dot_general(a, b, (((0,), (0,)), ((), ())), preferred_element_type=F32)


def _modulated_norm(x, g, shift, scale):
    ms = jnp.mean(x * x, axis=-1, keepdims=True)
    y = x * lax.rsqrt(ms + EPS) * g
    return y * (1.0 + scale) + shift


def _ada_kernel(c_ref, w_ref, b_ref, o_ref):
    c = c_ref[...]
    ca = (c * jax.nn.sigmoid(c)).astype(BF16)
    o_ref[0] = _dot(ca, w_ref[0].astype(BF16)) + b_ref[0]


def _ada_call(c, w_ada, b_ada):
    depth, d, n = w_ada.shape
    bsz = c.shape[0]
    tn = 512
    return pl.pallas_call(
        _ada_kernel,
        grid=(depth, n // tn),
        in_specs=[pl.BlockSpec((bsz, d), lambda l, j: (0, 0)),
                  pl.BlockSpec((1, d, tn), lambda l, j: (l, 0, j)),
                  pl.BlockSpec((1, 1, tn), lambda l, j: (l, 0, j))],
        out_specs=pl.BlockSpec((1, bsz, tn), lambda l, j: (l, 0, j)),
        out_shape=jax.ShapeDtypeStruct((depth, bsz, n), F32),
        compiler_params=_cparams(2),
        name="ada_cond",
    )(c, w_ada, b_ada.reshape(depth, 1, n))


def _in_kernel(x_ref, mod_ref, g_ref, w_ref, b_ref, wt_ref, bt_ref, rope_ref, ropet_ref, cw_ref,
               uconv_ref, ukv_ref, ksw_ref, qt_ref, vt_ref, gt_ref, carry_ref, stagek_ref, stagev_ref):
    i = pl.program_id(1)
    x = x_ref[0]
    tm = x.shape[0]
    mod = mod_ref[0]
    hb = _modulated_norm(x, g_ref[...], mod[0:1], mod[1:2]).astype(BF16)

    def seg(a, b):
        return _dot(hb, w_ref[:, a:b]) + b_ref[:, a:b]

    @pl.when(i == 0)
    def _():
        carry_ref[...] = jnp.zeros_like(carry_ref)

    v = seg(_C_CC, _C_CU) * seg(_C_CU, _C_KC)
    prev = carry_ref[...]
    row = lax.broadcasted_iota(jnp.int32, v.shape, 0)
    p1 = jnp.broadcast_to(prev[7:8], v.shape)
    p2 = jnp.broadcast_to(prev[6:7], v.shape)
    v1 = jnp.where(row == 0, p1, pltpu.roll(v, 1, 0))
    v2 = jnp.where(row == 0, p2, jnp.where(row == 1, p1, pltpu.roll(v, 2, 0)))
    cw = cw_ref[...]
    yc = cw[0:1] * v2 + cw[1:2] * v1 + cw[2:3] * v
    carry_ref[...] = v[tm - 8:tm]
    uconv_ref[0] = (seg(_C_CB, _C_CC) * yc).astype(BF16)

    cos2, sa2, sb2 = rope_ref[0, 0], rope_ref[0, 1], rope_ref[0, 2]
    lane = lax.broadcasted_iota(jnp.int32, cos2.shape, 1)
    low = lane < HEAD_DIM
    cos1, sa1, sb1 = jnp.where(low, cos2, 0.0), jnp.where(low, sa2, 0.0), jnp.where(low, sb2, 0.0)

    def rope(z, c, sa, sb):
        return z * c + pltpu.roll(z, LANE - HEAD_DIM // 2, 1) * sa + pltpu.roll(z, HEAD_DIM // 2, 1) * sb

    zkv = seg(_C_KC, _C_KS)
    stagek_ref[...] = rope(zkv[:, 0:KV_W], cos2, sa2, sb2)
    stagev_ref[...] = zkv[:, KV_W:2 * KV_W]
    n_ch = tm // CMP_D
    low_ch = lax.broadcasted_iota(jnp.int32, (n_ch, LANE), 1) < HEAD_DIM
    for a, stage_ref in enumerate((stagek_ref, stagev_ref)):
        for mm in range(CMP_D // 2):
            p0 = stage_ref[pl.ds(2 * mm, n_ch, stride=CMP_D), :]
            p1 = stage_ref[pl.ds(2 * mm + 1, n_ch, stride=CMP_D), :]
            lanes = slice(mm * LANE, (mm + 1) * LANE)
            ukv_ref[0, a, 0, :, lanes] = jnp.where(low_ch, p0, pltpu.roll(p1, HEAD_DIM, 1)).astype(BF16)
            ukv_ref[0, a, 1, :, lanes] = jnp.where(low_ch, pltpu.roll(p0, HEAD_DIM, 1), p1).astype(BF16)
    for br in range(2):
        zk = seg(_C_KS + br * N_KV * LANE, _C_KS + (br + 1) * N_KV * LANE)
        for g in range(N_KV):
            ksw_ref[0, br * N_KV + g] = rope(zk[:, g * LANE:(g + 1) * LANE], cos1, sa1, sb1).astype(BF16)

    zt = _dot_nt(wt_ref[...], hb) + bt_ref[...]
    ct, st = ropet_ref[0, 0], ropet_ref[0, 1]
    half = HEAD_DIM // 2
    scale = HEAD_DIM ** -0.5 * LOG2_E
    for h in range(N_HEADS):
        x1 = zt[_R_Q + h * HEAD_DIM:_R_Q + h * HEAD_DIM + half]
        x2 = zt[_R_Q + h * HEAD_DIM + half:_R_Q + (h + 1) * HEAD_DIM]
        qt_ref[0, h, 0:half, :] = ((x1 * ct - x2 * st) * scale).astype(BF16)
        qt_ref[0, h, half:HEAD_DIM, :] = ((x2 * ct + x1 * st) * scale).astype(BF16)
    for r in range(2 * N_KV):
        vt_ref[0, r, 0:HEAD_DIM, :] = zt[_R_VS + r * HEAD_DIM:_R_VS + (r + 1) * HEAD_DIM].astype(BF16)
        vt_ref[0, r, HEAD_DIM:V_ROWS, :] = jnp.ones((V_ROWS - HEAD_DIM, tm), BF16)
    for gi in range(N_KV):
        gt_ref[0, gi] = jax.nn.sigmoid(zt[_R_G + gi * GATE_ROWS:_R_G + (gi + 1) * GATE_ROWS])


def _in_call(x, mod, g, l, w_main, b_main, w_t, b_t, rope_tab, rope_tab_t, conv_w, tm):
    bsz, s, d = x.shape
    return pl.pallas_call(
        _in_kernel,
        grid=(bsz, s // tm),
        in_specs=[pl.BlockSpec((1, tm, d), lambda b, i: (b, i, 0)),
                  pl.BlockSpec((1, 6, d), lambda b, i: (b, 0, 0)),
                  _whole(1, d),
                  _layer(l, d, N_MAIN),
                  _layer(l, 1, N_MAIN),
                  _layer(l, N_TRANS, d),
                  _layer(l, N_TRANS, 1),
                  pl.BlockSpec((1, 3, tm, LANE), lambda b, i: (b, 0, i, 0)),
                  pl.BlockSpec((1, 2, HEAD_DIM // 2, tm), lambda b, i: (b, 0, 0, i)),
                  _layer(l, CONV_K, CONV_W)],
        out_specs=[pl.BlockSpec((1, tm, CONV_W), lambda b, i: (b, i, 0)),
                   pl.BlockSpec((1, 2, N_KV, tm // CMP_D, CMP_D * HEAD_DIM), lambda b, i: (b, 0, 0, i, 0)),
                   pl.BlockSpec((1, 2 * N_KV, tm, LANE), lambda b, i: (b, 0, i, 0)),
                   pl.BlockSpec((1, N_HEADS, HEAD_DIM, tm), lambda b, i: (b, 0, 0, i)),
                   pl.BlockSpec((1, 2 * N_KV, V_ROWS, tm), lambda b, i: (b, 0, 0, i)),
                   pl.BlockSpec((1, N_KV, GATE_ROWS, tm), lambda b, i: (b, 0, 0, i))],
        out_shape=[jax.ShapeDtypeStruct((bsz, s, CONV_W), BF16),
                   jax.ShapeDtypeStruct((bsz, 2, N_KV, s // CMP_D, CMP_D * HEAD_DIM), BF16),
                   jax.ShapeDtypeStruct((bsz, 2 * N_KV, s, LANE), BF16),
                   jax.ShapeDtypeStruct((bsz, N_HEADS, HEAD_DIM, s), BF16),
                   jax.ShapeDtypeStruct((bsz, 2 * N_KV, V_ROWS, s), BF16),
                   jax.ShapeDtypeStruct((bsz, N_KV, GATE_ROWS, s), F32)],
        scratch_shapes=[pltpu.VMEM((8, CONV_W), F32), pltpu.VMEM((tm, LANE), F32), pltpu.VMEM((tm, LANE), F32)],
        compiler_params=_cparams(2),
        name="in_proj",
    )(x, mod, g, w_main, b_main, w_t, b_t, rope_tab, rope_tab_t, conv_w)


def _cmp_kernel(uk_ref, uv_ref, w1k_ref, w1v_ref, pek_ref, pev_ref, w2k_ref, w2vt_ref,
                kcn_ref, vct_ref, *, nb, nchunk):
    m = nb * N_KV * nchunk
    kh = w1k_ref.shape[0] // 2

    def hidden(u_ref, w1_ref, pe_ref):
        u = jnp.concatenate([u_ref[j, 0, g] for j in range(nb) for g in range(N_KV)], axis=0)
        pe = pe_ref[...]
        pe_hi = pe.astype(BF16)
        pe_lo = (pe - pe_hi.astype(F32)).astype(BF16)
        w1 = w1_ref[...]
        pterm = (_dot(pe_hi, w1) + _dot(pe_lo, w1))[0:1]
        a = _dot(u, w1[0:kh])
        bm = _dot(u, w1[kh:2 * kh])
        hid = a + pltpu.roll(bm, m - 1, 0) + pterm
        return (hid * jax.nn.sigmoid(hid)).astype(BF16)

    hk = hidden(uk_ref, w1k_ref, pek_ref)
    hv = hidden(uv_ref, w1v_ref, pev_ref)
    kc_all = _dot(hk, w2k_ref[...]).astype(BF16)
    for j in range(nb):
        for g in range(N_KV):
            r0 = (j * N_KV + g) * nchunk
            kcn_ref[j, g] = kc_all[r0:r0 + nchunk]
            vct_ref[j, g] = _dot_nt(w2vt_ref[...], hv[r0:r0 + nchunk]).astype(BF16)


def _cmp_call(ukv, l, w1k, w1v, pek, pev, w2k, w2vt, nb):
    bsz, _, _, nchunk, feat = ukv.shape
    return pl.pallas_call(
        functools.partial(_cmp_kernel, nb=nb, nchunk=nchunk),
        grid=(bsz // nb,),
        in_specs=[pl.BlockSpec((nb, 1, N_KV, nchunk, feat), lambda j: (j, 0, 0, 0, 0)),
                  pl.BlockSpec((nb, 1, N_KV, nchunk, feat), lambda j: (j, 1, 0, 0, 0)),
                  _layer(l, 2 * feat, CMP_HIDDEN),
                  _layer(l, 2 * feat, CMP_HIDDEN),
                  _layer(l, 8, 2 * feat),
                  _layer(l, 8, 2 * feat),
                  _layer(l, CMP_HIDDEN, LANE),
                  _layer(l, HEAD_DIM, CMP_HIDDEN)],
        out_specs=[pl.BlockSpec((nb, N_KV, nchunk, LANE), lambda j: (j, 0, 0, 0)),
                   pl.BlockSpec((nb, N_KV, HEAD_DIM, nchunk), lambda j: (j, 0, 0, 0))],
        out_shape=[jax.ShapeDtypeStruct((bsz, N_KV, nchunk, LANE), BF16),
                   jax.ShapeDtypeStruct((bsz, N_KV, HEAD_DIM, nchunk), BF16)],
        compiler_params=_cparams(1),
        name="cmp_mlp",
    )(ukv, ukv, w1k, w1v, pek, pev, w2k, w2vt)


def _attn_kernel(qt_ref, kcn_ref, vct_ref, ks_ref, kw_ref, vst_ref, vwt_ref, gt_ref, ovt_ref, en_ref,
                 o_ref, m_s, acc_s, mw_s, accw_s, ocmp_s,
                 sa_s, sb_s, wa_s, wb_s, mxa_s, mxb_s, mxwa_s, mxwb_s, *, tq, ns, n_sel):
    i = pl.program_id(2)
    cols = HPG * tq
    qt = jnp.concatenate([qt_ref[0, h] for h in range(HPG)], axis=1)
    q_plain = jnp.concatenate([qt, jnp.zeros_like(qt)], axis=0)
    q_in_tile = lax.broadcasted_iota(jnp.int32, (1, cols), 1) & (tq - 1)
    t_col = i * tq + q_in_tile

    krow = lax.broadcasted_iota(jnp.int32, (tq, cols), 0)
    causal = krow <= q_in_tile
    band = krow > q_in_tile
    lane_k = lax.broadcasted_iota(jnp.int32, (tq, LANE), 1)

    def score(k_tile, q_aug, buf, mask):
        sc_ref, mx_ref = buf
        sc = _dot(k_tile, q_aug)
        if mask is not None:
            sc = jnp.where(mask, sc, NEG)
        sc_ref[...] = sc
        mx_ref[...] = jnp.max(sc, axis=0, keepdims=True)

    def accumulate(buf, vt_tile, m_r, acc_r):
        sc_ref, mx_ref = buf
        m_old = m_r[...]
        m_new = jnp.maximum(m_old, mx_ref[...])
        alpha = jnp.exp2(m_old - m_new)
        pt = jnp.exp2(sc_ref[...] - m_new)
        acc_r[...] = alpha * acc_r[...] + _dot(vt_tile, pt.astype(BF16))
        m_r[...] = m_new

    def normalized(acc_r):
        acc = acc_r[...]
        return acc[0:HEAD_DIM] / acc[HEAD_DIM:HEAD_DIM + 1]

    for m_r, acc_r in ((m_s, acc_s), (mw_s, accw_s)):
        m_r[...] = jnp.full_like(m_r, NEG)
        acc_r[...] = jnp.zeros_like(acc_r)

    n_back = WINDOW // tq
    brow = lax.broadcasted_iota(jnp.int32, (HEAD_DIM, cols), 0)
    wbias = jnp.where((brow >= 1) & (brow <= n_back) & (brow > i), SEL_BIAS, 0.0).astype(BF16)
    q_win = jnp.concatenate([qt, wbias], axis=0)
    wbufs = ((wa_s, mxwa_s), (wb_s, mxwb_s))

    def win_score(back, mask):
        k0 = pl.multiple_of(jnp.maximum(i - back, 0) * tq, tq)
        kaug = jnp.where(lane_k == HEAD_DIM + back, 1.0, kw_ref[0, 0, pl.ds(k0, tq), :])
        score(kaug, q_win, wbufs[back & 1], mask)

    def win_accumulate(back):
        k0 = pl.multiple_of(jnp.maximum(i - back, 0) * tq, tq)
        accumulate(wbufs[back & 1], vwt_ref[0, 0, :, pl.ds(k0, tq)], mw_s, accw_s)

    nct = kcn_ref.shape[2]
    s = _dot(kcn_ref[0, 0], q_plain)
    crow = lax.broadcasted_iota(jnp.int32, (nct, cols), 0)
    cmask = (crow * CMP_D + (CMP_L - 1)) <= t_col
    s = jnp.where(cmask, s, NEG)
    mx = jnp.max(s, axis=0, keepdims=True)
    e = jnp.where(cmask, jnp.exp2(s - mx), 0.0)
    den = jnp.sum(e, axis=0, keepdims=True)
    p = e / jnp.where(den > 0.0, den, 1.0)
    ocmp_s[...] = _dot(vct_ref[0, 0], p.astype(BF16))

    psum = p[:, 0:tq]
    for h in range(1, HPG):
        psum = psum + p[:, h * tq:(h + 1) * tq]
    p_hi = psum.astype(BF16)
    p_lo = (psum - p_hi.astype(F32)).astype(BF16)
    imp = _dot(ovt_ref[...], p_hi) + _dot(ovt_ref[...], p_lo)
    jrow = lax.broadcasted_iota(jnp.int32, (HEAD_DIM, tq), 0)
    jrow_f = jrow.astype(F32)
    bt = (i * tq + lax.broadcasted_iota(jnp.int32, (HEAD_DIM, tq), 1)) // SEL_L
    in_range = jrow < ns
    forced = (jrow == 0) | (jrow == bt) | (jrow == bt - 1)
    val = jnp.where(jrow <= bt, imp + jnp.where(forced, FORCE_BONUS, 0.0), -1.0)
    val = jnp.where(in_range, val, -3.0)
    sel = jnp.zeros((HEAD_DIM, tq), F32)
    for _ in range(n_sel):
        top = jnp.max(val, axis=0, keepdims=True)
        first = jnp.min(jnp.where(val == top, jrow_f, 1e3), axis=0, keepdims=True)
        hit = jrow_f == first
        sel = jnp.where(hit, 1.0, sel)
        val = jnp.where(hit, -5.0, val)
    bias = jnp.where(in_range & (sel == 0.0), SEL_BIAS, 0.0).astype(BF16)
    q_sel = jnp.concatenate([qt, jnp.concatenate([bias] * HPG, axis=1)], axis=0)

    win_score(0, causal)
    for back in range(1, n_back + 1):
        win_score(back, band if back == n_back else None)
        win_accumulate(back - 1)
    win_accumulate(n_back)

    buf_a, buf_b = (sa_s, mxa_s), (sb_s, mxb_s)

    def sel_score(kt, mask, buf):
        k0 = pl.multiple_of(kt * tq, tq)
        kaug = jnp.where(lane_k < HEAD_DIM, ks_ref[0, 0, pl.ds(k0, tq), :], en_ref[pl.ds(k0, tq), :])
        score(kaug, q_sel, buf, mask)

    def sel_accumulate(kt, buf):
        k0 = pl.multiple_of(kt * tq, tq)
        accumulate(buf, vst_ref[0, 0, :, pl.ds(k0, tq)], m_s, acc_s)

    sel_score(0, causal | (i > 0), buf_a)

    def sel_pair(j, carry):
        t = 2 * j
        sel_score(t + 1, None, buf_b)
        sel_accumulate(t, buf_a)
        sel_score(t + 2, None, buf_a)
        sel_accumulate(t + 1, buf_b)
        return carry

    lax.fori_loop(0, jnp.maximum(i - 1, 0) // 2, sel_pair, 0)

    @pl.when(i == 0)
    def _():
        sel_accumulate(0, buf_a)

    @pl.when((i & 1) == 1)
    def _():
        sel_score(i, causal, buf_b)
        sel_accumulate(i - 1, buf_a)
        sel_accumulate(i, buf_b)

    @pl.when(((i & 1) == 0) & (i > 0))
    def _():
        sel_score(i - 1, None, buf_b)
        sel_accumulate(i - 2, buf_a)
        sel_score(i, causal, buf_a)
        sel_accumulate(i - 1, buf_b)
        sel_accumulate(i, buf_a)

    gt = gt_ref[0, 0]
    o_cmp = ocmp_s[...]
    o_sel = normalized(acc_s)
    o_win = normalized(accw_s)
    for h in range(HPG):
        c0, c1 = h * tq, (h + 1) * tq
        o_h = (gt[h:h + 1] * o_cmp[:, c0:c1] + gt[SUBLANE + h:SUBLANE + h + 1] * o_sel[:, c0:c1]
               + gt[2 * SUBLANE + h:2 * SUBLANE + h + 1] * o_win[:, c0:c1])
        o_ref[0, h * HEAD_DIM:(h + 1) * HEAD_DIM, :] = o_h.astype(BF16)


def _attn_call(qt, kcn, vct, ksw, vt, gt, ovt, en, tq):
    bsz, _, _, s = qt.shape
    ns = s // SEL_L
    cols = HPG * tq
    nct = kcn.shape[2]
    return pl.pallas_call(
        functools.partial(_attn_kernel, tq=tq, ns=ns, n_sel=min(N_SEL, ns)),
        grid=(bsz, N_KV, s // tq),
        in_specs=[pl.BlockSpec((1, HPG, HEAD_DIM, tq), lambda b, g, i: (b, g, 0, i)),
                  pl.BlockSpec((1, 1, nct, LANE), lambda b, g, i: (b, g, 0, 0)),
                  pl.BlockSpec((1, 1, HEAD_DIM, nct), lambda b, g, i: (b, g, 0, 0)),
                  pl.BlockSpec((1, 1, s, LANE), lambda b, g, i: (b, g, 0, 0)),
                  pl.BlockSpec((1, 1, s, LANE), lambda b, g, i: (b, N_KV + g, 0, 0)),
                  pl.BlockSpec((1, 1, V_ROWS, s), lambda b, g, i: (b, g, 0, 0)),
                  pl.BlockSpec((1, 1, V_ROWS, s), lambda b, g, i: (b, N_KV + g, 0, 0)),
                  pl.BlockSpec((1, 1, GATE_ROWS, tq), lambda b, g, i: (b, g, 0, i)),
                  pl.BlockSpec((HEAD_DIM, nct), lambda b, g, i: (0, 0)),
                  pl.BlockSpec((s, LANE), lambda b, g, i: (0, 0))],
        out_specs=pl.BlockSpec((1, HPG * HEAD_DIM, tq), lambda b, g, i: (b, g, i)),
        out_shape=jax.ShapeDtypeStruct((bsz, ATTN_W, s), BF16),
        scratch_shapes=[pltpu.VMEM((1, cols), F32), pltpu.VMEM((V_ROWS, cols), F32),
                        pltpu.VMEM((1, cols), F32), pltpu.VMEM((V_ROWS, cols), F32),
                        pltpu.VMEM((HEAD_DIM, cols), F32),
                        pltpu.VMEM((tq, cols), F32), pltpu.VMEM((tq, cols), F32),
                        pltpu.VMEM((tq, cols), F32), pltpu.VMEM((tq, cols), F32),
                        pltpu.VMEM((1, cols), F32), pltpu.VMEM((1, cols), F32),
                        pltpu.VMEM((1, cols), F32), pltpu.VMEM((1, cols), F32)],
        compiler_params=_cparams(3),
        name="nsa_attn",
    )(qt, kcn, vct, ksw, ksw, vt, vt, gt, ovt, en)


def _merge_kernel(x_ref, mod_ref, g_ref, uconv_ref, ot_ref, wmg_ref, bmg_ref, wuc_ref, wua_ref, wo_ref, out_ref):
    x = x_ref[0]
    mod = mod_ref[0]
    d = x.shape[-1]
    hb = _modulated_norm(x, g_ref[...], mod[0:1], mod[1:2]).astype(BF16)
    y_conv = _dot(uconv_ref[0], wuc_ref[...])
    y_attn = _dot_tn(ot_ref[0], wua_ref[...])
    g_conv = jax.nn.sigmoid(_dot(hb, wmg_ref[:, 0:d]) + bmg_ref[:, 0:d])
    g_attn = jax.nn.sigmoid(_dot(hb, wmg_ref[:, d:2 * d]) + bmg_ref[:, d:2 * d])
    mixed = _dot((g_conv * y_conv + g_attn * y_attn).astype(BF16), wo_ref[...])
    out_ref[0] = x + mod[2:3] * mixed


def _merge_call(x, mod, g, uconv, ot, l, w_mg, b_mg, w_uc, w_ua, w_o, tm):
    bsz, s, d = x.shape
    return pl.pallas_call(
        _merge_kernel,
        grid=(bsz, s // tm),
        in_specs=[pl.BlockSpec((1, tm, d), lambda b, i: (b, i, 0)),
                  pl.BlockSpec((1, 6, d), lambda b, i: (b, 0, 0)),
                  _whole(1, d),
                  pl.BlockSpec((1, tm, CONV_W), lambda b, i: (b, i, 0)),
                  pl.BlockSpec((1, ATTN_W, tm), lambda b, i: (b, 0, i)),
                  _layer(l, d, 2 * d), _layer(l, 1, 2 * d), _layer(l, CONV_W, d), _layer(l, ATTN_W, d),
                  _layer(l, d, d)],
        out_specs=pl.BlockSpec((1, tm, d), lambda b, i: (b, i, 0)),
        out_shape=jax.ShapeDtypeStruct((bsz, s, d), F32),
        compiler_params=_cparams(2),
        name="merge_out",
    )(x, mod, g, uconv, ot, w_mg, b_mg, w_uc, w_ua, w_o)


def _mlp_kernel(x_ref, mod_ref, g_ref, w1_ref, w2_ref, *rest, tf):
    out_ref = rest[-1]
    x = x_ref[0]
    mod = mod_ref[0]
    hb = _modulated_norm(x, g_ref[...], mod[3:4], mod[4:5]).astype(BF16)
    acc = None
    for c in range(w1_ref.shape[1] // tf):
        a = jnp.maximum(_dot(hb, w1_ref[:, c * tf:(c + 1) * tf]), 0.0)
        part = _dot((a * a).astype(BF16), w2_ref[c * tf:(c + 1) * tf, :])
        acc = part if acc is None else acc + part
    y = x + mod[5:6] * acc
    if len(rest) == 2:
        ms = jnp.mean(y * y, axis=-1, keepdims=True)
        y = y * lax.rsqrt(ms + EPS) * rest[0][...]
    out_ref[0] = y


def _mlp_call(x, mod, g, l, w1, w2, g_final, tm, tf):
    bsz, s, d = x.shape
    d_ff = w1.shape[-1]
    extra = () if g_final is None else (g_final,)
    return pl.pallas_call(
        functools.partial(_mlp_kernel, tf=tf),
        grid=(bsz, s // tm),
        in_specs=[pl.BlockSpec((1, tm, d), lambda b, i: (b, i, 0)),
                  pl.BlockSpec((1, 6, d), lambda b, i: (b, 0, 0)),
                  _whole(1, d), _layer(l, d, d_ff), _layer(l, d_ff, d)] + [_whole(1, d)] * len(extra),
        out_specs=pl.BlockSpec((1, tm, d), lambda b, i: (b, i, 0)),
        out_shape=jax.ShapeDtypeStruct((bsz, s, d), F32),
        compiler_params=_cparams(2),
        name="relu2_mlp",
    )(x, mod, g, w1, w2, *extra)


def _pack_weights(w_in, b_in, cmp_k_w2, cmp_v_w2):
    offs = np.concatenate([[0], np.cumsum(IN_SIZES)])
    col = lambda a, k: a[..., int(offs[k]):int(offs[k + 1])]

    def pad_groups(a):
        lead = a.shape[:-1]
        a = a.reshape(lead + (N_KV, HEAD_DIM))
        a = jnp.pad(a, [(0, 0)] * len(lead) + [(0, 0), (0, LANE - HEAD_DIM)])
        return a.reshape(lead + (N_KV * LANE,))

    def gate_slots(a):
        lead = a.shape[:-1]
        a = a.reshape(lead + (3, N_KV, HPG))
        a = jnp.moveaxis(a, -2, -3)
        a = jnp.pad(a, [(0, 0)] * (len(lead) + 2) + [(0, SUBLANE - HPG)])
        return a.reshape(lead + (N_KV * GATE_ROWS,))

    def main(a):
        return jnp.concatenate([col(a, 0), col(a, 1), col(a, 2), col(a, 4), col(a, 5),
                                pad_groups(col(a, 6)), pad_groups(col(a, 8))], axis=-1)

    def trans(a):
        return jnp.concatenate([col(a, 3), col(a, 7), col(a, 9), gate_slots(col(a, 10))], axis=-1)

    w_main = main(w_in).astype(BF16)
    b_main = main(b_in)[:, None, :]
    w_t = jnp.swapaxes(trans(w_in), 1, 2).astype(BF16)
    b_t = trans(b_in)[:, :, None]
    w_mg = col(w_in, 11).astype(BF16)
    b_mg = col(b_in, 11)[:, None, :]
    w2k = jnp.pad(cmp_k_w2, [(0, 0), (0, 0), (0, LANE - HEAD_DIM)]).astype(BF16)
    w2vt = jnp.swapaxes(cmp_v_w2, 1, 2).astype(BF16)
    return w_main, b_main, w_t, b_t, w_mg, b_mg, w2k, w2vt


def _rope_tables(positions):
    half = HEAD_DIM // 2
    inv_freq = ROPE_THETA ** (-jnp.arange(half, dtype=F32) * 2.0 / HEAD_DIM)
    pos = positions.astype(F32)
    ang = pos[..., None] * jnp.tile(inv_freq, LANE // half)
    cos, sin = jnp.cos(ang), jnp.sin(ang)
    first = (np.arange(LANE) % HEAD_DIM) < half
    lane_tab = jnp.stack([cos, jnp.where(first, -sin, 0.0), jnp.where(first, 0.0, sin)], axis=1)
    ang_t = inv_freq[None, :, None] * pos[:, None, :]
    t_tab = jnp.stack([jnp.cos(ang_t), jnp.sin(ang_t)], axis=1)
    return lane_tab, t_tab


def _static_tables(s, nchunk):
    ns = s // SEL_L
    cstart = np.arange(nchunk) * CMP_D
    jj = np.arange(ns)
    overlap = (cstart[:, None] < (jj[None, :] + 1) * SEL_L) & (cstart[:, None] + CMP_L > jj[None, :] * SEL_L)
    ovt = np.zeros((HEAD_DIM, nchunk), np.float32)
    ovt[:ns] = overlap.T
    en = np.zeros((s, LANE), np.float32)
    en[np.arange(s), HEAD_DIM + np.arange(s) // SEL_L] = 1.0
    return jnp.asarray(ovt, BF16), jnp.asarray(en, BF16)


def kernel(x, c, positions, w_ada, b_ada, g_mix, w_in, b_in, conv_w, cmp_k_w1, cmp_k_w2, cmp_k_pe, cmp_v_w1, cmp_v_w2, cmp_v_pe, w_up_conv, w_up_attn, w_o, g_mlp, w_ff1, w_ff2, g_final):
    bsz, s, d = x.shape
    depth = w_ada.shape[0]
    nchunk = s // CMP_D
    tm = 512
    tq = 256
    nb = 4 if bsz % 4 == 0 else 1
    assert s % tm == 0 and s % tq == 0 and WINDOW % tq == 0 and nchunk == LANE and s // SEL_L <= HEAD_DIM

    w_main, b_main, w_t, b_t, w_mg, b_mg, w2k, w2vt = _pack_weights(w_in, b_in, cmp_k_w2, cmp_v_w2)
    w1k, w1v = cmp_k_w1.astype(BF16), cmp_v_w1.astype(BF16)
    pek = jnp.broadcast_to(cmp_k_pe.reshape(depth, 1, CMP_L * HEAD_DIM), (depth, 8, CMP_L * HEAD_DIM))
    pev = jnp.broadcast_to(cmp_v_pe.reshape(depth, 1, CMP_L * HEAD_DIM), (depth, 8, CMP_L * HEAD_DIM))
    w_uc, w_ua, w_ob = w_up_conv.astype(BF16), w_up_attn.astype(BF16), w_o.astype(BF16)
    w_f1, w_f2 = w_ff1.astype(BF16), w_ff2.astype(BF16)
    rope_tab, rope_tab_t = _rope_tables(positions)
    ovt, en = _static_tables(s, nchunk)

    cond = _ada_call(c, w_ada, b_ada).reshape(depth, bsz, 6, d)

    for l in range(depth):
        mod = cond[l]
        uconv, ukv, ksw, qt, vt, gt = _in_call(x, mod, g_mix[l][None], l, w_main, b_main, w_t, b_t,
                                               rope_tab, rope_tab_t, conv_w, tm)
        kcn, vct = _cmp_call(ukv, l, w1k, w1v, pek, pev, w2k, w2vt, nb)
        ot = _attn_call(qt, kcn, vct, ksw, vt, gt, ovt, en, tq)
        x = _merge_call(x, mod, g_mix[l][None], uconv, ot, l, w_mg, b_mg, w_uc, w_ua, w_ob, tm)
        x = _mlp_call(x, mod, g_mlp[l][None], l, w_f1, w_f2, g_final[None] if l == depth - 1 else None, tm, 1024)
    return x
```

```python
import functools

import jax
import jax.numpy as jnp
import numpy as np
from jax import lax
from jax.experimental import pallas as pl
from jax.experimental.pallas import tpu as pltpu

D_MODEL = 1024
CONV_W = 512
CONV_K = 3
N_HEADS = 8
HEAD_DIM = 64
N_KV = 2
HPG = N_HEADS // N_KV
ATTN_W = N_HEADS * HEAD_DIM
KV_W = N_KV * HEAD_DIM
CMP_L = 32
CMP_D = 16
CMP_HIDDEN = 4 * HEAD_DIM
SEL_L = 64
N_SEL = 8
WINDOW = 512
FORCE_BONUS = 1e4
NEG = -1e30
D_FF = 4 * D_MODEL
ROPE_THETA = 10000.0
EPS = 1e-6
IN_SIZES = (CONV_W, CONV_W, CONV_W, ATTN_W, KV_W, KV_W, KV_W, KV_W, KV_W, KV_W, 3 * N_HEADS, 2 * D_MODEL)

LANE = 128
SUBLANE = 8
SEL_BIAS = -1e30
LOG2_E = 1.4426950408889634
VMEM_LIMIT = 56 * 1024 * 1024
GATE_ROWS = 3 * SUBLANE
V_ROWS = HEAD_DIM + 2 * SUBLANE

_C_CB, _C_CC, _C_CU = 0, CONV_W, 2 * CONV_W
_C_KC = 3 * CONV_W
_C_VC = _C_KC + KV_W
_C_KS = _C_VC + KV_W
_C_KW = _C_KS + N_KV * LANE
N_MAIN = _C_KW + N_KV * LANE
_R_Q = 0
_R_VS = ATTN_W
_R_VW = _R_VS + KV_W
_R_G = _R_VW + KV_W
N_TRANS = _R_G + N_KV * GATE_ROWS

F32 = jnp.float32
BF16 = jnp.bfloat16


def _cparams(n_grid):
    return pltpu.CompilerParams(dimension_semantics=("arbitrary",) * n_grid,
                                vmem_limit_bytes=VMEM_LIMIT)


def _whole(*shape):
    return pl.BlockSpec(shape, lambda *_: (0,) * len(shape))


def _layer(l, *shape):
    return pl.BlockSpec((None,) + shape, lambda *_: (l,) + (0,) * len(shape))


def _dot(a, b):
    return jnp.dot(a, b, preferred_element_type=F32)


def _dot_nt(a, b):
    return lax.dot_general(a, b, (((1,), (1,)), ((), ())), preferred_element_type=F32)


def _dot_tn(a, b):
    return lax.dot_general(a, b, (((0,), (0,)), ((), ())), preferred_element_type=F32)


def _modulated_norm(x, g, shift, scale):
    ms = jnp.mean(x * x, axis=-1, keepdims=True)
    y = x * lax.rsqrt(ms + EPS) * g
    return y * (1.0 + scale) + shift


def _ada_kernel(c_ref, w_ref, b_ref, o_ref):
    c = c_ref[...]
    ca = (c * jax.nn.sigmoid(c)).astype(BF16)
    o_ref[0] = _dot(ca, w_ref[0].astype(BF16)) + b_ref[0]


def _ada_call(c, w_ada, b_ada):
    depth, d, n = w_ada.shape
    bsz = c.shape[0]
    tn = 512
    return pl.pallas_call(
        _ada_kernel,
        grid=(depth, n // tn),
        in_specs=[pl.BlockSpec((bsz, d), lambda l, j: (0, 0)),
                  pl.BlockSpec((1, d, tn), lambda l, j: (l, 0, j)),
                  pl.BlockSpec((1, 1, tn), lambda l, j: (l, 0, j))],
        out_specs=pl.BlockSpec((1, bsz, tn), lambda l, j: (l, 0, j)),
        out_shape=jax.ShapeDtypeStruct((depth, bsz, n), F32),
        compiler_params=_cparams(2),
        name="ada_cond",
    )(c, w_ada, b_ada.reshape(depth, 1, n))


def _in_kernel(x_ref, mod_ref, g_ref, w_ref, b_ref, wt_ref, bt_ref, rope_ref, ropet_ref, cw_ref,
               uconv_ref, ukv_ref, ksw_ref, qt_ref, vt_ref, gt_ref, carry_ref, stagek_ref, stagev_ref):
    i = pl.program_id(1)
    x = x_ref[0]
    tm = x.shape[0]
    mod = mod_ref[0]
    hb = _modulated_norm(x, g_ref[...], mod[0:1], mod[1:2]).astype(BF16)

    def seg(a, b):
        return _dot(hb, w_ref[:, a:b]) + b_ref[:, a:b]

    @pl.when(i == 0)
    def _():
        carry_ref[...] = jnp.zeros_like(carry_ref)

    v = seg(_C_CC, _C_CU) * seg(_C_CU, _C_KC)
    prev = carry_ref[...]
    row = lax.broadcasted_iota(jnp.int32, v.shape, 0)
    p1 = jnp.broadcast_to(prev[7:8], v.shape)
    p2 = jnp.broadcast_to(prev[6:7], v.shape)
    v1 = jnp.where(row == 0, p1, pltpu.roll(v, 1, 0))
    v2 = jnp.where(row == 0, p2, jnp.where(row == 1, p1, pltpu.roll(v, 2, 0)))
    cw = cw_ref[...]
    yc = cw[0:1] * v2 + cw[1:2] * v1 + cw[2:3] * v
    carry_ref[...] = v[tm - 8:tm]
    uconv_ref[0] = (seg(_C_CB, _C_CC) * yc).astype(BF16)

    cos2, sa2, sb2 = rope_ref[0, 0], rope_ref[0, 1], rope_ref[0, 2]
    lane = lax.broadcasted_iota(jnp.int32, cos2.shape, 1)
    low = lane < HEAD_DIM
    cos1, sa1, sb1 = jnp.where(low, cos2, 0.0), jnp.where(low, sa2, 0.0), jnp.where(low, sb2, 0.0)

    def rope(z, c, sa, sb):
        return z * c + pltpu.roll(z, LANE - HEAD_DIM // 2, 1) * sa + pltpu.roll(z, HEAD_DIM // 2, 1) * sb

    zkv = seg(_C_KC, _C_KS)
    stagek_ref[...] = rope(zkv[:, 0:KV_W], cos2, sa2, sb2)
    stagev_ref[...] = zkv[:, KV_W:2 * KV_W]
    n_ch = tm // CMP_D
    low_ch = lax.broadcasted_iota(jnp.int32, (n_ch, LANE), 1) < HEAD_DIM
    for a, stage_ref in enumerate((stagek_ref, stagev_ref)):
        for mm in range(CMP_D // 2):
            p0 = stage_ref[pl.ds(2 * mm, n_ch, stride=CMP_D), :]
            p1 = stage_ref[pl.ds(2 * mm + 1, n_ch, stride=CMP_D), :]
            lanes = slice(mm * LANE, (mm + 1) * LANE)
            ukv_ref[0, a, 0, :, lanes] = jnp.where(low_ch, p0, pltpu.roll(p1, HEAD_DIM, 1)).astype(BF16)
            ukv_ref[0, a, 1, :, lanes] = jnp.where(low_ch, pltpu.roll(p0, HEAD_DIM, 1), p1).astype(BF16)
    for br in range(2):
        zk = seg(_C_KS + br * N_KV * LANE, _C_KS + (br + 1) * N_KV * LANE)
        for g in range(N_KV):
            ksw_ref[0, br * N_KV + g] = rope(zk[:, g * LANE:(g + 1) * LANE], cos1, sa1, sb1).astype(BF16)

    zt = _dot_nt(wt_ref[...], hb) + bt_ref[...]
    ct, st = ropet_ref[0, 0], ropet_ref[0, 1]
    half = HEAD_DIM // 2
    scale = HEAD_DIM ** -0.5 * LOG2_E
    for h in range(N_HEADS):
        x1 = zt[_R_Q + h * HEAD_DIM:_R_Q + h * HEAD_DIM + half]
        x2 = zt[_R_Q + h * HEAD_DIM + half:_R_Q + (h + 1) * HEAD_DIM]
        qt_ref[0, h, 0:half, :] = ((x1 * ct - x2 * st) * scale).astype(BF16)
        qt_ref[0, h, half:HEAD_DIM, :] = ((x2 * ct + x1 * st) * scale).astype(BF16)
    for r in range(2 * N_KV):
        vt_ref[0, r, 0:HEAD_DIM, :] = zt[_R_VS + r * HEAD_DIM:_R_VS + (r + 1) * HEAD_DIM].astype(BF16)
        vt_ref[0, r, HEAD_DIM:V_ROWS, :] = jnp.ones((V_ROWS - HEAD_DIM, tm), BF16)
    for gi in range(N_KV):
        gt_ref[0, gi] = jax.nn.sigmoid(zt[_R_G + gi * GATE_ROWS:_R_G + (gi + 1) * GATE_ROWS])


def _in_call(x, mod, g, l, w_main, b_main, w_t, b_t, rope_tab, rope_tab_t, conv_w, tm):
    bsz, s, d = x.shape
    return pl.pallas_call(
        _in_kernel,
        grid=(bsz, s // tm),
        in_specs=[pl.BlockSpec((1, tm, d), lambda b, i: (b, i, 0)),
                  pl.BlockSpec((1, 6, d), lambda b, i: (b, 0, 0)),
                  _whole(1, d),
                  _layer(l, d, N_MAIN),
                  _layer(l, 1, N_MAIN),
                  _layer(l, N_TRANS, d),
                  _layer(l, N_TRANS, 1),
                  pl.BlockSpec((1, 3, tm, LANE), lambda b, i: (b, 0, i, 0)),
                  pl.BlockSpec((1, 2, HEAD_DIM // 2, tm), lambda b, i: (b, 0, 0, i)),
                  _layer(l, CONV_K, CONV_W)],
        out_specs=[pl.BlockSpec((1, tm, CONV_W), lambda b, i: (b, i, 0)),
                   pl.BlockSpec((1, 2, N_KV, tm // CMP_D, CMP_D * HEAD_DIM), lambda b, i: (b, 0, 0, i, 0)),
                   pl.BlockSpec((1, 2 * N_KV, tm, LANE), lambda b, i: (b, 0, i, 0)),
                   pl.BlockSpec((1, N_HEADS, HEAD_DIM, tm), lambda b, i: (b, 0, 0, i)),
                   pl.BlockSpec((1, 2 * N_KV, V_ROWS, tm), lambda b, i: (b, 0, 0, i)),
                   pl.BlockSpec((1, N_KV, GATE_ROWS, tm), lambda b, i: (b, 0, 0, i))],
        out_shape=[jax.ShapeDtypeStruct((bsz, s, CONV_W), BF16),
                   jax.ShapeDtypeStruct((bsz, 2, N_KV, s // CMP_D, CMP_D * HEAD_DIM), BF16),
                   jax.ShapeDtypeStruct((bsz, 2 * N_KV, s, LANE), BF16),
                   jax.ShapeDtypeStruct((bsz, N_HEADS, HEAD_DIM, s), BF16),
                   jax.ShapeDtypeStruct((bsz, 2 * N_KV, V_ROWS, s), BF16),
                   jax.ShapeDtypeStruct((bsz, N_KV, GATE_ROWS, s), F32)],
        scratch_shapes=[pltpu.VMEM((8, CONV_W), F32), pltpu.VMEM((tm, LANE), F32), pltpu.VMEM((tm, LANE), F32)],
        compiler_params=_cparams(2),
        name="in_proj",
    )(x, mod, g, w_main, b_main, w_t, b_t, rope_tab, rope_tab_t, conv_w)


def _cmp_kernel(uk_ref, uv_ref, w1k_ref, w1v_ref, pek_ref, pev_ref, w2k_ref, w2vt_ref,
                kcn_ref, vct_ref, *, nb, nchunk):
    m = nb * N_KV * nchunk
    kh = w1k_ref.shape[0] // 2

    def hidden(u_ref, w1_ref, pe_ref):
        u = jnp.concatenate([u_ref[j, 0, g] for j in range(nb) for g in range(N_KV)], axis=0)
        pe = pe_ref[...]
        pe_hi = pe.astype(BF16)
        pe_lo = (pe - pe_hi.astype(F32)).astype(BF16)
        w1 = w1_ref[...]
        pterm = (_dot(pe_hi, w1) + _dot(pe_lo, w1))[0:1]
        a = _dot(u, w1[0:kh])
        bm = _dot(u, w1[kh:2 * kh])
        hid = a + pltpu.roll(bm, m - 1, 0) + pterm
        return (hid * jax.nn.sigmoid(hid)).astype(BF16)

    hk = hidden(uk_ref, w1k_ref, pek_ref)
    hv = hidden(uv_ref, w1v_ref, pev_ref)
    kc_all = _dot(hk, w2k_ref[...]).astype(BF16)
    for j in range(nb):
        for g in range(N_KV):
            r0 = (j * N_KV + g) * nchunk
            kcn_ref[j, g] = kc_all[r0:r0 + nchunk]
            vct_ref[j, g] = _dot_nt(w2vt_ref[...], hv[r0:r0 + nchunk]).astype(BF16)


def _cmp_call(ukv, l, w1k, w1v, pek, pev, w2k, w2vt, nb):
    bsz, _, _, nchunk, feat = ukv.shape
    return pl.pallas_call(
        functools.partial(_cmp_kernel, nb=nb, nchunk=nchunk),
        grid=(bsz // nb,),
        in_specs=[pl.BlockSpec((nb, 1, N_KV, nchunk, feat), lambda j: (j, 0, 0, 0, 0)),
                  pl.BlockSpec((nb, 1, N_KV, nchunk, feat), lambda j: (j, 1, 0, 0, 0)),
                  _layer(l, 2 * feat, CMP_HIDDEN),
                  _layer(l, 2 * feat, CMP_HIDDEN),
                  _layer(l, 8, 2 * feat),
                  _layer(l, 8, 2 * feat),
                  _layer(l, CMP_HIDDEN, LANE),
                  _layer(l, HEAD_DIM, CMP_HIDDEN)],
        out_specs=[pl.BlockSpec((nb, N_KV, nchunk, LANE), lambda j: (j, 0, 0, 0)),
                   pl.BlockSpec((nb, N_KV, HEAD_DIM, nchunk), lambda j: (j, 0, 0, 0))],
        out_shape=[jax.ShapeDtypeStruct((bsz, N_KV, nchunk, LANE), BF16),
                   jax.ShapeDtypeStruct((bsz, N_KV, HEAD_DIM, nchunk), BF16)],
        compiler_params=_cparams(1),
        name="cmp_mlp",
    )(ukv, ukv, w1k, w1v, pek, pev, w2k, w2vt)


def _attn_kernel(qt_ref, kcn_ref, vct_ref, ks_ref, kw_ref, vst_ref, vwt_ref, gt_ref, ovt_ref, en_ref,
                 o_ref, m_s, acc_s, mw_s, accw_s, ocmp_s,
                 sa_s, sb_s, wa_s, wb_s, mxa_s, mxb_s, mxwa_s, mxwb_s, *, tq, ns, n_sel):
    i = pl.program_id(2)
    cols = HPG * tq
    qt = jnp.concatenate([qt_ref[0, h] for h in range(HPG)], axis=1)
    q_plain = jnp.concatenate([qt, jnp.zeros_like(qt)], axis=0)
    q_in_tile = lax.broadcasted_iota(jnp.int32, (1, cols), 1) & (tq - 1)
    t_col = i * tq + q_in_tile

    krow = lax.broadcasted_iota(jnp.int32, (tq, cols), 0)
    causal = krow <= q_in_tile
    band = krow > q_in_tile
    lane_k = lax.broadcasted_iota(jnp.int32, (tq, LANE), 1)

    def score(k_tile, q_aug, buf, mask):
        sc_ref, mx_ref = buf
        sc = _dot(k_tile, q_aug)
        if mask is not None:
            sc = jnp.where(mask, sc, NEG)
        sc_ref[...] = sc
        mx_ref[...] = jnp.max(sc, axis=0, keepdims=True)

    def accumulate(buf, vt_tile, m_r, acc_r):
        sc_ref, mx_ref = buf
        m_old = m_r[...]
        m_new = jnp.maximum(m_old, mx_ref[...])
        alpha = jnp.exp2(m_old - m_new)
        pt = jnp.exp2(sc_ref[...] - m_new)
        acc_r[...] = alpha * acc_r[...] + _dot(vt_tile, pt.astype(BF16))
        m_r[...] = m_new

    def normalized(acc_r):
        acc = acc_r[...]
        return acc[0:HEAD_DIM] * (1.0 / acc[HEAD_DIM:HEAD_DIM + 1])

    for m_r, acc_r in ((m_s, acc_s), (mw_s, accw_s)):
        m_r[...] = jnp.full_like(m_r, NEG)
        acc_r[...] = jnp.zeros_like(acc_r)

    n_back = WINDOW // tq
    brow = lax.broadcasted_iota(jnp.int32, (HEAD_DIM, cols), 0)
    wbias = jnp.where((brow >= 1) & (brow <= n_back) & (brow > i), SEL_BIAS, 0.0).astype(BF16)
    q_win = jnp.concatenate([qt, wbias], axis=0)
    wbufs = ((wa_s, mxwa_s), (wb_s, mxwb_s))

    def win_score(back, mask):
        k0 = pl.multiple_of(jnp.maximum(i - back, 0) * tq, tq)
        kaug = jnp.where(lane_k == HEAD_DIM + back, 1.0, kw_ref[0, 0, pl.ds(k0, tq), :])
        score(kaug, q_win, wbufs[back & 1], mask)

    def win_accumulate(back):
        k0 = pl.multiple_of(jnp.maximum(i - back, 0) * tq, tq)
        accumulate(wbufs[back & 1], vwt_ref[0, 0, :, pl.ds(k0, tq)], mw_s, accw_s)

    nct = kcn_ref.shape[2]
    s = _dot(kcn_ref[0, 0], q_plain)
    crow = lax.broadcasted_iota(jnp.int32, (nct, cols), 0)
    cmask = (crow * CMP_D + (CMP_L - 1)) <= t_col
    s = jnp.where(cmask, s, NEG)
    mx = jnp.max(s, axis=0, keepdims=True)
    e = jnp.exp2(s - mx)
    den = jnp.sum(e, axis=0, keepdims=True)
    p = e * jnp.where(mx > 0.5 * NEG, 1.0 / den, 0.0)
    ocmp_s[...] = _dot(vct_ref[0, 0], p.astype(BF16))

    psum = p[:, 0:tq]
    for h in range(1, HPG):
        psum = psum + p[:, h * tq:(h + 1) * tq]
    p_hi = psum.astype(BF16)
    p_lo = (psum - p_hi.astype(F32)).astype(BF16)
    imp = _dot(ovt_ref[...], p_hi) + _dot(ovt_ref[...], p_lo)
    jrow = lax.broadcasted_iota(jnp.int32, (HEAD_DIM, tq), 0)
    jrow_f = jrow.astype(F32)
    bt = (i * tq + lax.broadcasted_iota(jnp.int32, (HEAD_DIM, tq), 1)) // SEL_L
    in_range = jrow < ns
    forced = (jrow == 0) | (jrow == bt) | (jrow == bt - 1)
    val = jnp.where(jrow <= bt, imp + jnp.where(forced, FORCE_BONUS, 0.0), -1.0)
    val = jnp.where(in_range, val, -3.0)
    sel = jnp.zeros((HEAD_DIM, tq), F32)
    for _ in range(n_sel):
        top = jnp.max(val, axis=0, keepdims=True)
        first = jnp.min(jnp.where(val == top, jrow_f, 1e3), axis=0, keepdims=True)
        hit = jrow_f == first
        sel = jnp.where(hit, 1.0, sel)
        val = jnp.where(hit, -5.0, val)
    bias = jnp.where(in_range & (sel == 0.0), SEL_BIAS, 0.0).astype(BF16)
    q_sel = jnp.concatenate([qt, jnp.concatenate([bias] * HPG, axis=1)], axis=0)

    win_score(0, causal)
    for back in range(1, n_back + 1):
        win_score(back, band if back == n_back else None)
        win_accumulate(back - 1)
    win_accumulate(n_back)

    buf_a, buf_b = (sa_s, mxa_s), (sb_s, mxb_s)

    def sel_score(kt, mask, buf):
        k0 = pl.multiple_of(kt * tq, tq)
        kaug = jnp.where(lane_k < HEAD_DIM, ks_ref[0, 0, pl.ds(k0, tq), :], en_ref[pl.ds(k0, tq), :])
        score(kaug, q_sel, buf, mask)

    def sel_accumulate(kt, buf):
        k0 = pl.multiple_of(kt * tq, tq)
        accumulate(buf, vst_ref[0, 0, :, pl.ds(k0, tq)], m_s, acc_s)

    sel_score(0, causal | (i > 0), buf_a)

    def sel_pair(j, carry):
        t = 2 * j
        sel_score(t + 1, None, buf_b)
        sel_accumulate(t, buf_a)
        sel_score(t + 2, None, buf_a)
        sel_accumulate(t + 1, buf_b)
        return carry

    lax.fori_loop(0, jnp.maximum(i - 1, 0) // 2, sel_pair, 0)

    @pl.when(i == 0)
    def _():
        sel_accumulate(0, buf_a)

    @pl.when((i & 1) == 1)
    def _():
        sel_score(i, causal, buf_b)
        sel_accumulate(i - 1, buf_a)
        sel_accumulate(i, buf_b)

    @pl.when(((i & 1) == 0) & (i > 0))
    def _():
        sel_score(i - 1, None, buf_b)
        sel_accumulate(i - 2, buf_a)
        sel_score(i, causal, buf_a)
        sel_accumulate(i - 1, buf_b)
        sel_accumulate(i, buf_a)

    gt = gt_ref[0, 0]
    o_cmp = ocmp_s[...]
    o_sel = normalized(acc_s)
    o_win = normalized(accw_s)
    for h in range(HPG):
        c0, c1 = h * tq, (h + 1) * tq
        o_h = (gt[h:h + 1] * o_cmp[:, c0:c1] + gt[SUBLANE + h:SUBLANE + h + 1] * o_sel[:, c0:c1]
               + gt[2 * SUBLANE + h:2 * SUBLANE + h + 1] * o_win[:, c0:c1])
        o_ref[0, h * HEAD_DIM:(h + 1) * HEAD_DIM, :] = o_h.astype(BF16)


def _attn_call(qt, kcn, vct, ksw, vt, gt, ovt, en, tq):
    bsz, _, _, s = qt.shape
    ns = s // SEL_L
    cols = HPG * tq
    nct = kcn.shape[2]
    return pl.pallas_call(
        functools.partial(_attn_kernel, tq=tq, ns=ns, n_sel=min(N_SEL, ns)),
        grid=(bsz, N_KV, s // tq),
        in_specs=[pl.BlockSpec((1, HPG, HEAD_DIM, tq), lambda b, g, i: (b, g, 0, i)),
                  pl.BlockSpec((1, 1, nct, LANE), lambda b, g, i: (b, g, 0, 0)),
                  pl.BlockSpec((1, 1, HEAD_DIM, nct), lambda b, g, i: (b, g, 0, 0)),
                  pl.BlockSpec((1, 1, s, LANE), lambda b, g, i: (b, g, 0, 0)),
                  pl.BlockSpec((1, 1, s, LANE), lambda b, g, i: (b, N_KV + g, 0, 0)),
                  pl.BlockSpec((1, 1, V_ROWS, s), lambda b, g, i: (b, g, 0, 0)),
                  pl.BlockSpec((1, 1, V_ROWS, s), lambda b, g, i: (b, N_KV + g, 0, 0)),
                  pl.BlockSpec((1, 1, GATE_ROWS, tq), lambda b, g, i: (b, g, 0, i)),
                  pl.BlockSpec((HEAD_DIM, nct), lambda b, g, i: (0, 0)),
                  pl.BlockSpec((s, LANE), lambda b, g, i: (0, 0))],
        out_specs=pl.BlockSpec((1, HPG * HEAD_DIM, tq), lambda b, g, i: (b, g, i)),
        out_shape=jax.ShapeDtypeStruct((bsz, ATTN_W, s), BF16),
        scratch_shapes=[pltpu.VMEM((1, cols), F32), pltpu.VMEM((V_ROWS, cols), F32),
                        pltpu.VMEM((1, cols), F32), pltpu.VMEM((V_ROWS, cols), F32),
                        pltpu.VMEM((HEAD_DIM, cols), F32),
                        pltpu.VMEM((tq, cols), F32), pltpu.VMEM((tq, cols), F32),
                        pltpu.VMEM((tq, cols), F32), pltpu.VMEM((tq, cols), F32),
                        pltpu.VMEM((1, cols), F32), pltpu.VMEM((1, cols), F32),
                        pltpu.VMEM((1, cols), F32), pltpu.VMEM((1, cols), F32)],
        compiler_params=_cparams(3),
        name="nsa_attn",
    )(qt, kcn, vct, ksw, ksw, vt, vt, gt, ovt, en)


def _merge_kernel(x_ref, mod_ref, g_ref, uconv_ref, ot_ref, wmg_ref, bmg_ref, wuc_ref, wua_ref, wo_ref, out_ref):
    x = x_ref[0]
    mod = mod_ref[0]
    d = x.shape[-1]
    hb = _modulated_norm(x, g_ref[...], mod[0:1], mod[1:2]).astype(BF16)
    y_conv = _dot(uconv_ref[0], wuc_ref[...])
    y_attn = _dot_tn(ot_ref[0], wua_ref[...])
    g_conv = jax.nn.sigmoid(_dot(hb, wmg_ref[:, 0:d]) + bmg_ref[:, 0:d])
    g_attn = jax.nn.sigmoid(_dot(hb, wmg_ref[:, d:2 * d]) + bmg_ref[:, d:2 * d])
    mixed = _dot((g_conv * y_conv + g_attn * y_attn).astype(BF16), wo_ref[...])
    out_ref[0] = x + mod[2:3] * mixed


def _merge_call(x, mod, g, uconv, ot, l, w_mg, b_mg, w_uc, w_ua, w_o, tm):
    bsz, s, d = x.shape
    return pl.pallas_call(
        _merge_kernel,
        grid=(bsz, s // tm),
        in_specs=[pl.BlockSpec((1, tm, d), lambda b, i: (b, i, 0)),
                  pl.BlockSpec((1, 6, d), lambda b, i: (b, 0, 0)),
                  _whole(1, d),
                  pl.BlockSpec((1, tm, CONV_W), lambda b, i: (b, i, 0)),
                  pl.BlockSpec((1, ATTN_W, tm), lambda b, i: (b, 0, i)),
                  _layer(l, d, 2 * d), _layer(l, 1, 2 * d), _layer(l, CONV_W, d), _layer(l, ATTN_W, d),
                  _layer(l, d, d)],
        out_specs=pl.BlockSpec((1, tm, d), lambda b, i: (b, i, 0)),
        out_shape=jax.ShapeDtypeStruct((bsz, s, d), F32),
        compiler_params=_cparams(2),
        name="merge_out",
    )(x, mod, g, uconv, ot, w_mg, b_mg, w_uc, w_ua, w_o)


def _mlp_kernel(x_ref, mod_ref, g_ref, w1_ref, w2_ref, *rest, tf):
    out_ref = rest[-1]
    x = x_ref[0]
    mod = mod_ref[0]
    hb = _modulated_norm(x, g_ref[...], mod[3:4], mod[4:5]).astype(BF16)
    acc = None
    for c in range(w1_ref.shape[1] // tf):
        a = jnp.maximum(_dot(hb, w1_ref[:, c * tf:(c + 1) * tf]), 0.0)
        part = _dot((a * a).astype(BF16), w2_ref[c * tf:(c + 1) * tf, :])
        acc = part if acc is None else acc + part
    y = x + mod[5:6] * acc
    if len(rest) == 2:
        ms = jnp.mean(y * y, axis=-1, keepdims=True)
        y = y * lax.rsqrt(ms + EPS) * rest[0][...]
    out_ref[0] = y


def _mlp_call(x, mod, g, l, w1, w2, g_final, tm, tf):
    bsz, s, d = x.shape
    d_ff = w1.shape[-1]
    extra = () if g_final is None else (g_final,)
    return pl.pallas_call(
        functools.partial(_mlp_kernel, tf=tf),
        grid=(bsz, s // tm),
        in_specs=[pl.BlockSpec((1, tm, d), lambda b, i: (b, i, 0)),
                  pl.BlockSpec((1, 6, d), lambda b, i: (b, 0, 0)),
                  _whole(1, d), _layer(l, d, d_ff), _layer(l, d_ff, d)] + [_whole(1, d)] * len(extra),
        out_specs=pl.BlockSpec((1, tm, d), lambda b, i: (b, i, 0)),
        out_shape=jax.ShapeDtypeStruct((bsz, s, d), F32),
        compiler_params=_cparams(2),
        name="relu2_mlp",
    )(x, mod, g, w1, w2, *extra)


def _pack_weights(w_in, b_in, cmp_k_w2, cmp_v_w2):
    offs = np.concatenate([[0], np.cumsum(IN_SIZES)])
    col = lambda a, k: a[..., int(offs[k]):int(offs[k + 1])]

    def pad_groups(a):
        lead = a.shape[:-1]
        a = a.reshape(lead + (N_KV, HEAD_DIM))
        a = jnp.pad(a, [(0, 0)] * len(lead) + [(0, 0), (0, LANE - HEAD_DIM)])
        return a.reshape(lead + (N_KV * LANE,))

    def gate_slots(a):
        lead = a.shape[:-1]
        a = a.reshape(lead + (3, N_KV, HPG))
        a = jnp.moveaxis(a, -2, -3)
        a = jnp.pad(a, [(0, 0)] * (len(lead) + 2) + [(0, SUBLANE - HPG)])
        return a.reshape(lead + (N_KV * GATE_ROWS,))

    def main(a):
        return jnp.concatenate([col(a, 0), col(a, 1), col(a, 2), col(a, 4), col(a, 5),
                                pad_groups(col(a, 6)), pad_groups(col(a, 8))], axis=-1)

    def trans(a):
        return jnp.concatenate([col(a, 3), col(a, 7), col(a, 9), gate_slots(col(a, 10))], axis=-1)

    w_main = main(w_in).astype(BF16)
    b_main = main(b_in)[:, None, :]
    w_t = jnp.swapaxes(trans(w_in), 1, 2).astype(BF16)
    b_t = trans(b_in)[:, :, None]
    w_mg = col(w_in, 11).astype(BF16)
    b_mg = col(b_in, 11)[:, None, :]
    w2k = jnp.pad(cmp_k_w2, [(0, 0), (0, 0), (0, LANE - HEAD_DIM)]).astype(BF16)
    w2vt = jnp.swapaxes(cmp_v_w2, 1, 2).astype(BF16)
    return w_main, b_main, w_t, b_t, w_mg, b_mg, w2k, w2vt


def _rope_tables(positions):
    half = HEAD_DIM // 2
    inv_freq = ROPE_THETA ** (-jnp.arange(half, dtype=F32) * 2.0 / HEAD_DIM)
    pos = positions.astype(F32)
    ang = pos[..., None] * jnp.tile(inv_freq, LANE // half)
    cos, sin = jnp.cos(ang), jnp.sin(ang)
    first = (np.arange(LANE) % HEAD_DIM) < half
    lane_tab = jnp.stack([cos, jnp.where(first, -sin, 0.0), jnp.where(first, 0.0, sin)], axis=1)
    ang_t = inv_freq[None, :, None] * pos[:, None, :]
    t_tab = jnp.stack([jnp.cos(ang_t), jnp.sin(ang_t)], axis=1)
    return lane_tab, t_tab


def _static_tables(s, nchunk):
    ns = s // SEL_L
    cstart = np.arange(nchunk) * CMP_D
    jj = np.arange(ns)
    overlap = (cstart[:, None] < (jj[None, :] + 1) * SEL_L) & (cstart[:, None] + CMP_L > jj[None, :] * SEL_L)
    ovt = np.zeros((HEAD_DIM, nchunk), np.float32)
    ovt[:ns] = overlap.T
    en = np.zeros((s, LANE), np.float32)
    en[np.arange(s), HEAD_DIM + np.arange(s) // SEL_L] = 1.0
    return jnp.asarray(ovt, BF16), jnp.asarray(en, BF16)


def kernel(x, c, positions, w_ada, b_ada, g_mix, w_in, b_in, conv_w, cmp_k_w1, cmp_k_w2, cmp_k_pe, cmp_v_w1, cmp_v_w2, cmp_v_pe, w_up_conv, w_up_attn, w_o, g_mlp, w_ff1, w_ff2, g_final):
    bsz, s, d = x.shape
    depth = w_ada.shape[0]
    nchunk = s // CMP_D
    tm = 1024
    tq = 256
    nb = 4 if bsz % 4 == 0 else 1
    assert s % tm == 0 and s % tq == 0 and WINDOW % tq == 0 and nchunk == LANE and s // SEL_L <= HEAD_DIM

    w_main, b_main, w_t, b_t, w_mg, b_mg, w2k, w2vt = _pack_weights(w_in, b_in, cmp_k_w2, cmp_v_w2)
    w1k, w1v = cmp_k_w1.astype(BF16), cmp_v_w1.astype(BF16)
    pek = jnp.broadcast_to(cmp_k_pe.reshape(depth, 1, CMP_L * HEAD_DIM), (depth, 8, CMP_L * HEAD_DIM))
    pev = jnp.broadcast_to(cmp_v_pe.reshape(depth, 1, CMP_L * HEAD_DIM), (depth, 8, CMP_L * HEAD_DIM))
    w_uc, w_ua, w_ob = w_up_conv.astype(BF16), w_up_attn.astype(BF16), w_o.astype(BF16)
    w_f1, w_f2 = w_ff1.astype(BF16), w_ff2.astype(BF16)
    rope_tab, rope_tab_t = _rope_tables(positions)
    ovt, en = _static_tables(s, nchunk)

    cond = _ada_call(c, w_ada, b_ada).reshape(depth, bsz, 6, d)

    for l in range(depth):
        mod = cond[l]
        uconv, ukv, ksw, qt, vt, gt = _in_call(x, mod, g_mix[l][None], l, w_main, b_main, w_t, b_t,
                                               rope_tab, rope_tab_t, conv_w, tm)
        kcn, vct = _cmp_call(ukv, l, w1k, w1v, pek, pev, w2k, w2vt, nb)
        ot = _attn_call(qt, kcn, vct, ksw, vt, gt, ovt, en, tq)
        x = _merge_call(x, mod, g_mix[l][None], uconv, ot, l, w_mg, b_mg, w_uc, w_ua, w_ob, tm)
        x = _mlp_call(x, mod, g_mlp[l][None], l, w_f1, w_f2, g_final[None] if l == depth - 1 else None, tm, 1024)
    return x
```

```python
import functools

import jax
import jax.numpy as jnp
import numpy as np
from jax import lax
from jax.experimental import pallas as pl
from jax.experimental.pallas import tpu as pltpu

D_MODEL = 1024
CONV_W = 512
CONV_K = 3
N_HEADS = 8
HEAD_DIM = 64
N_KV = 2
HPG = N_HEADS // N_KV
ATTN_W = N_HEADS * HEAD_DIM
KV_W = N_KV * HEAD_DIM
CMP_L = 32
CMP_D = 16
CMP_HIDDEN = 4 * HEAD_DIM
SEL_L = 64
N_SEL = 8
WINDOW = 512
FORCE_BONUS = 1e4
NEG = -1e30
D_FF = 4 * D_MODEL
ROPE_THETA = 10000.0
EPS = 1e-6
IN_SIZES = (CONV_W, CONV_W, CONV_W, ATTN_W, KV_W, KV_W, KV_W, KV_W, KV_W, KV_W, 3 * N_HEADS, 2 * D_MODEL)

LANE = 128
SUBLANE = 8
SEL_BIAS = -1e30
LOG2_E = 1.4426950408889634
VMEM_LIMIT = 56 * 1024 * 1024
GATE_ROWS = 3 * SUBLANE
V_ROWS = HEAD_DIM + 2 * SUBLANE

_C_CB, _C_CC, _C_CU = 0, CONV_W, 2 * CONV_W
_C_KC = 3 * CONV_W
_C_VC = _C_KC + KV_W
_C_KS = _C_VC + KV_W
_C_KW = _C_KS + N_KV * LANE
N_MAIN = _C_KW + N_KV * LANE
_R_Q = 0
_R_VS = ATTN_W
_R_VW = _R_VS + KV_W
_R_G = _R_VW + KV_W
N_TRANS = _R_G + N_KV * GATE_ROWS

F32 = jnp.float32
BF16 = jnp.bfloat16


def _cparams(n_grid):
    return pltpu.CompilerParams(dimension_semantics=("arbitrary",) * n_grid,
                                vmem_limit_bytes=VMEM_LIMIT)


def _whole(*shape):
    return pl.BlockSpec(shape, lambda *_: (0,) * len(shape))


def _layer(l, *shape):
    return pl.BlockSpec((None,) + shape, lambda *_: (l,) + (0,) * len(shape))


def _dot(a, b):
    return jnp.dot(a, b, preferred_element_type=F32)


def _dot_nt(a, b):
    return lax.dot_general(a, b, (((1,), (1,)), ((), ())), preferred_element_type=F32)


def _dot_tn(a, b):
    return lax.dot_general(a, b, (((0,), (0,)), ((), ())), preferred_element_type=F32)


def _modulated_norm(x, g, shift, scale):
    ms = jnp.mean(x * x, axis=-1, keepdims=True)
    y = x * lax.rsqrt(ms + EPS) * g
    return y * (1.0 + scale) + shift


def _ada_kernel(c_ref, w_ref, b_ref, o_ref):
    c = c_ref[...]
    ca = (c * jax.nn.sigmoid(c)).astype(BF16)
    o_ref[0] = _dot(ca, w_ref[0].astype(BF16)) + b_ref[0]


def _ada_call(c, w_ada, b_ada):
    depth, d, n = w_ada.shape
    bsz = c.shape[0]
    tn = 512
    return pl.pallas_call(
        _ada_kernel,
        grid=(depth, n // tn),
        in_specs=[pl.BlockSpec((bsz, d), lambda l, j: (0, 0)),
                  pl.BlockSpec((1, d, tn), lambda l, j: (l, 0, j)),
                  pl.BlockSpec((1, 1, tn), lambda l, j: (l, 0, j))],
        out_specs=pl.BlockSpec((1, bsz, tn), lambda l, j: (l, 0, j)),
        out_shape=jax.ShapeDtypeStruct((depth, bsz, n), F32),
        compiler_params=_cparams(2),
        name="ada_cond",
    )(c, w_ada, b_ada.reshape(depth, 1, n))


def _in_kernel(x_ref, mod_ref, g_ref, w_ref, b_ref, wt_ref, bt_ref, rope_ref, ropet_ref, cw_ref,
               uconv_ref, ukv_ref, ksw_ref, qt_ref, vt_ref, gt_ref, carry_ref, stagek_ref, stagev_ref):
    i = pl.program_id(1)
    x = x_ref[0]
    tm = x.shape[0]
    mod = mod_ref[0]
    hb = _modulated_norm(x, g_ref[...], mod[0:1], mod[1:2]).astype(BF16)

    def seg(a, b):
        return _dot(hb, w_ref[:, a:b]) + b_ref[:, a:b]

    @pl.when(i == 0)
    def _():
        carry_ref[...] = jnp.zeros_like(carry_ref)

    v = seg(_C_CC, _C_CU) * seg(_C_CU, _C_KC)
    prev = carry_ref[...]
    row = lax.broadcasted_iota(jnp.int32, v.shape, 0)
    p1 = jnp.broadcast_to(prev[7:8], v.shape)
    p2 = jnp.broadcast_to(prev[6:7], v.shape)
    v1 = jnp.where(row == 0, p1, pltpu.roll(v, 1, 0))
    v2 = jnp.where(row == 0, p2, jnp.where(row == 1, p1, pltpu.roll(v, 2, 0)))
    cw = cw_ref[...]
    yc = cw[0:1] * v2 + cw[1:2] * v1 + cw[2:3] * v
    carry_ref[...] = v[tm - 8:tm]
    uconv_ref[0] = (seg(_C_CB, _C_CC) * yc).astype(BF16)

    cos2, sa2, sb2 = rope_ref[0, 0], rope_ref[0, 1], rope_ref[0, 2]
    lane = lax.broadcasted_iota(jnp.int32, cos2.shape, 1)
    low = lane < HEAD_DIM
    cos1, sa1, sb1 = jnp.where(low, cos2, 0.0), jnp.where(low, sa2, 0.0), jnp.where(low, sb2, 0.0)

    def rope(z, c, sa, sb):
        return z * c + pltpu.roll(z, LANE - HEAD_DIM // 2, 1) * sa + pltpu.roll(z, HEAD_DIM // 2, 1) * sb

    zkv = seg(_C_KC, _C_KS)
    stagek_ref[...] = rope(zkv[:, 0:KV_W], cos2, sa2, sb2)
    stagev_ref[...] = zkv[:, KV_W:2 * KV_W]
    n_ch = tm // CMP_D
    low_ch = lax.broadcasted_iota(jnp.int32, (n_ch, LANE), 1) < HEAD_DIM
    for a, stage_ref in enumerate((stagek_ref, stagev_ref)):
        for mm in range(CMP_D // 2):
            p0 = stage_ref[pl.ds(2 * mm, n_ch, stride=CMP_D), :]
            p1 = stage_ref[pl.ds(2 * mm + 1, n_ch, stride=CMP_D), :]
            lanes = slice(mm * LANE, (mm + 1) * LANE)
            ukv_ref[0, a, 0, :, lanes] = jnp.where(low_ch, p0, pltpu.roll(p1, HEAD_DIM, 1)).astype(BF16)
            ukv_ref[0, a, 1, :, lanes] = jnp.where(low_ch, pltpu.roll(p0, HEAD_DIM, 1), p1).astype(BF16)
    for br in range(2):
        zk = seg(_C_KS + br * N_KV * LANE, _C_KS + (br + 1) * N_KV * LANE)
        for g in range(N_KV):
            ksw_ref[0, br * N_KV + g] = rope(zk[:, g * LANE:(g + 1) * LANE], cos1, sa1, sb1).astype(BF16)

    zt = _dot_nt(wt_ref[...], hb) + bt_ref[...]
    ct, st = ropet_ref[0, 0], ropet_ref[0, 1]
    half = HEAD_DIM // 2
    scale = HEAD_DIM ** -0.5 * LOG2_E
    for h in range(N_HEADS):
        x1 = zt[_R_Q + h * HEAD_DIM:_R_Q + h * HEAD_DIM + half]
        x2 = zt[_R_Q + h * HEAD_DIM + half:_R_Q + (h + 1) * HEAD_DIM]
        qt_ref[0, h, 0:half, :] = ((x1 * ct - x2 * st) * scale).astype(BF16)
        qt_ref[0, h, half:HEAD_DIM, :] = ((x2 * ct + x1 * st) * scale).astype(BF16)
    for r in range(2 * N_KV):
        vt_ref[0, r, 0:HEAD_DIM, :] = zt[_R_VS + r * HEAD_DIM:_R_VS + (r + 1) * HEAD_DIM].astype(BF16)
        vt_ref[0, r, HEAD_DIM:V_ROWS, :] = jnp.ones((V_ROWS - HEAD_DIM, tm), BF16)
    for gi in range(N_KV):
        gt_ref[0, gi] = jax.nn.sigmoid(zt[_R_G + gi * GATE_ROWS:_R_G + (gi + 1) * GATE_ROWS])


def _in_call(x, mod, g, l, w_main, b_main, w_t, b_t, rope_tab, rope_tab_t, conv_w, tm):
    bsz, s, d = x.shape
    return pl.pallas_call(
        _in_kernel,
        grid=(bsz, s // tm),
        in_specs=[pl.BlockSpec((1, tm, d), lambda b, i: (b, i, 0)),
                  pl.BlockSpec((1, 6, d), lambda b, i: (b, 0, 0)),
                  _whole(1, d),
                  _layer(l, d, N_MAIN),
                  _layer(l, 1, N_MAIN),
                  _layer(l, N_TRANS, d),
                  _layer(l, N_TRANS, 1),
                  pl.BlockSpec((1, 3, tm, LANE), lambda b, i: (b, 0, i, 0)),
                  pl.BlockSpec((1, 2, HEAD_DIM // 2, tm), lambda b, i: (b, 0, 0, i)),
                  _layer(l, CONV_K, CONV_W)],
        out_specs=[pl.BlockSpec((1, tm, CONV_W), lambda b, i: (b, i, 0)),
                   pl.BlockSpec((1, 2, N_KV, tm // CMP_D, CMP_D * HEAD_DIM), lambda b, i: (b, 0, 0, i, 0)),
                   pl.BlockSpec((1, 2 * N_KV, tm, LANE), lambda b, i: (b, 0, i, 0)),
                   pl.BlockSpec((1, N_HEADS, HEAD_DIM, tm), lambda b, i: (b, 0, 0, i)),
                   pl.BlockSpec((1, 2 * N_KV, V_ROWS, tm), lambda b, i: (b, 0, 0, i)),
                   pl.BlockSpec((1, N_KV, GATE_ROWS, tm), lambda b, i: (b, 0, 0, i))],
        out_shape=[jax.ShapeDtypeStruct((bsz, s, CONV_W), BF16),
                   jax.ShapeDtypeStruct((bsz, 2, N_KV, s // CMP_D, CMP_D * HEAD_DIM), BF16),
                   jax.ShapeDtypeStruct((bsz, 2 * N_KV, s, LANE), BF16),
                   jax.ShapeDtypeStruct((bsz, N_HEADS, HEAD_DIM, s), BF16),
                   jax.ShapeDtypeStruct((bsz, 2 * N_KV, V_ROWS, s), BF16),
                   jax.ShapeDtypeStruct((bsz, N_KV, GATE_ROWS, s), F32)],
        scratch_shapes=[pltpu.VMEM((8, CONV_W), F32), pltpu.VMEM((tm, LANE), F32), pltpu.VMEM((tm, LANE), F32)],
        compiler_params=_cparams(2),
        name="in_proj",
    )(x, mod, g, w_main, b_main, w_t, b_t, rope_tab, rope_tab_t, conv_w)


def _cmp_kernel(uk_ref, uv_ref, w1k_ref, w1v_ref, pek_ref, pev_ref, w2k_ref, w2vt_ref,
                kcn_ref, vct_ref, *, nb, nchunk):
    m = nb * N_KV * nchunk
    kh = w1k_ref.shape[0] // 2

    def hidden(u_ref, w1_ref, pe_ref):
        u = jnp.concatenate([u_ref[j, 0, g] for j in range(nb) for g in range(N_KV)], axis=0)
        pe = pe_ref[...]
        pe_hi = pe.astype(BF16)
        pe_lo = (pe - pe_hi.astype(F32)).astype(BF16)
        w1 = w1_ref[...]
        pterm = (_dot(pe_hi, w1) + _dot(pe_lo, w1))[0:1]
        a = _dot(u, w1[0:kh])
        bm = _dot(u, w1[kh:2 * kh])
        hid = a + pltpu.roll(bm, m - 1, 0) + pterm
        return (hid * jax.nn.sigmoid(hid)).astype(BF16)

    hk = hidden(uk_ref, w1k_ref, pek_ref)
    hv = hidden(uv_ref, w1v_ref, pev_ref)
    kc_all = _dot(hk, w2k_ref[...]).astype(BF16)
    for j in range(nb):
        for g in range(N_KV):
            r0 = (j * N_KV + g) * nchunk
            kcn_ref[j, g] = kc_all[r0:r0 + nchunk]
            vct_ref[j, g] = _dot_nt(w2vt_ref[...], hv[r0:r0 + nchunk]).astype(BF16)


def _cmp_call(ukv, l, w1k, w1v, pek, pev, w2k, w2vt, nb):
    bsz, _, _, nchunk, feat = ukv.shape
    return pl.pallas_call(
        functools.partial(_cmp_kernel, nb=nb, nchunk=nchunk),
        grid=(bsz // nb,),
        in_specs=[pl.BlockSpec((nb, 1, N_KV, nchunk, feat), lambda j: (j, 0, 0, 0, 0)),
                  pl.BlockSpec((nb, 1, N_KV, nchunk, feat), lambda j: (j, 1, 0, 0, 0)),
                  _layer(l, 2 * feat, CMP_HIDDEN),
                  _layer(l, 2 * feat, CMP_HIDDEN),
                  _layer(l, 8, 2 * feat),
                  _layer(l, 8, 2 * feat),
                  _layer(l, CMP_HIDDEN, LANE),
                  _layer(l, HEAD_DIM, CMP_HIDDEN)],
        out_specs=[pl.BlockSpec((nb, N_KV, nchunk, LANE), lambda j: (j, 0, 0, 0)),
                   pl.BlockSpec((nb, N_KV, HEAD_DIM, nchunk), lambda j: (j, 0, 0, 0))],
        out_shape=[jax.ShapeDtypeStruct((bsz, N_KV, nchunk, LANE), BF16),
                   jax.ShapeDtypeStruct((bsz, N_KV, HEAD_DIM, nchunk), BF16)],
        compiler_params=_cparams(1),
        name="cmp_mlp",
    )(ukv, ukv, w1k, w1v, pek, pev, w2k, w2vt)


def _attn_kernel(qt_ref, kcn_ref, vct_ref, ks_ref, kw_ref, vst_ref, vwt_ref, gt_ref, ovt_ref, en_ref, cm_ref,
                 o_ref, m_s, acc_s, mw_s, accw_s, ocmp_s,
                 sa_s, sb_s, wa_s, wb_s, mxa_s, mxb_s, mxwa_s, mxwb_s, *, tq, ns, n_sel):
    i = pl.program_id(2)
    cols = HPG * tq
    qt = jnp.concatenate([qt_ref[0, h] for h in range(HPG)], axis=1)
    q_plain = jnp.concatenate([qt, jnp.zeros_like(qt)], axis=0)
    q_in_tile = lax.broadcasted_iota(jnp.int32, (1, cols), 1) & (tq - 1)
    t_col = i * tq + q_in_tile

    krow = lax.broadcasted_iota(jnp.int32, (tq, cols), 0)
    causal = krow <= q_in_tile
    lane_k = lax.broadcasted_iota(jnp.int32, (tq, LANE), 1)

    def score(k_tile, q_aug, buf, mask):
        sc_ref, mx_ref = buf
        sc = _dot(k_tile, q_aug)
        if mask is not None:
            sc = jnp.where(mask, sc, NEG)
        sc_ref[...] = sc
        mx_ref[...] = jnp.max(sc, axis=0, keepdims=True)

    def accumulate(buf, vt_tile, m_r, acc_r):
        sc_ref, mx_ref = buf
        m_old = m_r[...]
        m_new = jnp.maximum(m_old, mx_ref[...])
        alpha = jnp.exp2(m_old - m_new)
        pt = jnp.exp2(sc_ref[...] - m_new)
        acc_r[...] = alpha * acc_r[...] + _dot(vt_tile, pt.astype(BF16))
        m_r[...] = m_new

    def normalized(acc_r):
        acc = acc_r[...]
        return acc[0:HEAD_DIM] * (1.0 / acc[HEAD_DIM:HEAD_DIM + 1])

    for m_r, acc_r in ((m_s, acc_s), (mw_s, accw_s)):
        m_r[...] = jnp.full_like(m_r, NEG)
        acc_r[...] = jnp.zeros_like(acc_r)

    n_back = WINDOW // tq
    brow = lax.broadcasted_iota(jnp.int32, (HEAD_DIM, cols), 0)
    wbias = jnp.where((brow >= 1) & (brow <= n_back) & (brow > i), SEL_BIAS, 0.0).astype(BF16)
    q_win = jnp.concatenate([qt, wbias], axis=0)
    wbufs = ((wa_s, mxwa_s), (wb_s, mxwb_s))

    def win_keys(back):
        k0 = pl.multiple_of(jnp.maximum(i - back, 0) * tq, tq)
        return jnp.where(lane_k == HEAD_DIM + back, 1.0, kw_ref[0, 0, pl.ds(k0, tq), :])

    def win_values(back):
        k0 = pl.multiple_of(jnp.maximum(i - back, 0) * tq, tq)
        return vwt_ref[0, 0, :, pl.ds(k0, tq)]

    def win_edge_score():
        sc = jnp.where(causal, _dot(win_keys(0), q_win), _dot(win_keys(n_back), q_win))
        wa_s[...] = sc
        mxwa_s[...] = jnp.max(sc, axis=0, keepdims=True)

    def win_edge_accumulate():
        m_old = mw_s[...]
        m_new = jnp.maximum(m_old, mxwa_s[...])
        alpha = jnp.exp2(m_old - m_new)
        pt = jnp.exp2(wa_s[...] - m_new).astype(BF16)
        pt_diag = pt * cm_ref[...]
        accw_s[...] = (alpha * accw_s[...] + _dot(win_values(0), pt_diag)
                       + _dot(win_values(n_back), pt - pt_diag))
        mw_s[...] = m_new

    def win_score(back):
        score(win_keys(back), q_win, wbufs[back & 1], None)

    def win_accumulate(back):
        accumulate(wbufs[back & 1], win_values(back), mw_s, accw_s)

    sc0_plain = _dot(ks_ref[0, 0, 0:tq, :], q_plain)

    nct = kcn_ref.shape[2]
    s = _dot(kcn_ref[0, 0], q_plain)
    crow = lax.broadcasted_iota(jnp.int32, (nct, cols), 0)
    cmask = (crow * CMP_D + (CMP_L - 1)) <= t_col
    s = jnp.where(cmask, s, NEG)
    mx = jnp.max(s, axis=0, keepdims=True)
    e = jnp.exp2(s - mx)
    den = jnp.sum(e, axis=0, keepdims=True)
    p = e * jnp.where(mx > 0.5 * NEG, 1.0 / den, 0.0)
    ocmp_s[...] = _dot(vct_ref[0, 0], p.astype(BF16))

    psum = p[:, 0:tq]
    for h in range(1, HPG):
        psum = psum + p[:, h * tq:(h + 1) * tq]
    p_hi = psum.astype(BF16)
    p_lo = (psum - p_hi.astype(F32)).astype(BF16)
    nr = -(-ns // (2 * SUBLANE)) * (2 * SUBLANE)
    imp = (_dot(ovt_ref[...], p_hi) + _dot(ovt_ref[...], p_lo))[0:nr]
    jrow = lax.broadcasted_iota(jnp.int32, (nr, tq), 0)
    jrow_f = jrow.astype(F32)
    bt = (i * tq + lax.broadcasted_iota(jnp.int32, (nr, tq), 1)) // SEL_L
    in_range = jrow < ns
    forced = (jrow == 0) | (jrow == bt) | (jrow == bt - 1)
    val = jnp.where(forced | (jrow > bt), -1.0, imp)
    val = jnp.where(in_range, val, -3.0)
    sel = jnp.where(forced, 1.0, 0.0)
    for _ in range(n_sel - 3):
        top = jnp.max(val, axis=0, keepdims=True)
        first = jnp.min(jnp.where(val == top, jrow_f, 1e3), axis=0, keepdims=True)
        hit = jrow_f == first
        sel = jnp.where(hit, 1.0, sel)
        val = jnp.where(hit, -5.0, val)
    bias_f = jnp.where(in_range & (sel == 0.0), SEL_BIAS, 0.0)
    bias = bias_f.astype(BF16)
    if nr < HEAD_DIM:
        bias = jnp.concatenate([bias, jnp.zeros((HEAD_DIM - nr, tq), BF16)], axis=0)
    q_sel = jnp.concatenate([qt, jnp.concatenate([bias] * HPG, axis=1)], axis=0)

    win_edge_score()
    for back in range(1, n_back):
        win_score(back)
        if back == 1:
            win_edge_accumulate()
        else:
            win_accumulate(back - 1)
    if n_back == 1:
        win_edge_accumulate()
    else:
        win_accumulate(n_back - 1)

    buf_a, buf_b = (sa_s, mxa_s), (sb_s, mxb_s)

    def sel_score(kt, mask, buf):
        k0 = pl.multiple_of(kt * tq, tq)
        kaug = jnp.where(lane_k < HEAD_DIM, ks_ref[0, 0, pl.ds(k0, tq), :], en_ref[pl.ds(k0, tq), :])
        score(kaug, q_sel, buf, mask)

    def sel_accumulate(kt, buf):
        k0 = pl.multiple_of(kt * tq, tq)
        accumulate(buf, vst_ref[0, 0, :, pl.ds(k0, tq)], m_s, acc_s)

    bias_rows = []
    for j in range(tq // SEL_L):
        row = jnp.concatenate([bias_f[j:j + 1]] * HPG, axis=1)
        bias_rows.append(jnp.broadcast_to(row, (SEL_L, cols)))
    sc0 = jnp.where(causal | (i > 0), sc0_plain + jnp.concatenate(bias_rows, axis=0), NEG)
    sa_s[...] = sc0
    mxa_s[...] = jnp.max(sc0, axis=0, keepdims=True)

    def sel_pair(j, carry):
        t = 2 * j
        sel_score(t + 1, None, buf_b)
        sel_accumulate(t, buf_a)
        sel_score(t + 2, None, buf_a)
        sel_accumulate(t + 1, buf_b)
        return carry

    lax.fori_loop(0, jnp.maximum(i - 1, 0) // 2, sel_pair, 0)

    @pl.when(i == 0)
    def _():
        sel_accumulate(0, buf_a)

    @pl.when((i & 1) == 1)
    def _():
        sel_score(i, causal, buf_b)
        sel_accumulate(i - 1, buf_a)
        sel_accumulate(i, buf_b)

    @pl.when(((i & 1) == 0) & (i > 0))
    def _():
        sel_score(i - 1, None, buf_b)
        sel_accumulate(i - 2, buf_a)
        sel_score(i, causal, buf_a)
        sel_accumulate(i - 1, buf_b)
        sel_accumulate(i, buf_a)

    gt = gt_ref[0, 0]
    o_cmp = ocmp_s[...]
    o_sel = normalized(acc_s)
    o_win = normalized(accw_s)
    for h in range(HPG):
        c0, c1 = h * tq, (h + 1) * tq
        o_h = (gt[h:h + 1] * o_cmp[:, c0:c1] + gt[SUBLANE + h:SUBLANE + h + 1] * o_sel[:, c0:c1]
               + gt[2 * SUBLANE + h:2 * SUBLANE + h + 1] * o_win[:, c0:c1])
        o_ref[0, h * HEAD_DIM:(h + 1) * HEAD_DIM, :] = o_h.astype(BF16)


def _attn_call(qt, kcn, vct, ksw, vt, gt, ovt, en, cm, tq):
    bsz, _, _, s = qt.shape
    ns = s // SEL_L
    cols = HPG * tq
    nct = kcn.shape[2]
    return pl.pallas_call(
        functools.partial(_attn_kernel, tq=tq, ns=ns, n_sel=min(N_SEL, ns)),
        grid=(bsz, N_KV, s // tq),
        in_specs=[pl.BlockSpec((1, HPG, HEAD_DIM, tq), lambda b, g, i: (b, g, 0, i)),
                  pl.BlockSpec((1, 1, nct, LANE), lambda b, g, i: (b, g, 0, 0)),
                  pl.BlockSpec((1, 1, HEAD_DIM, nct), lambda b, g, i: (b, g, 0, 0)),
                  pl.BlockSpec((1, 1, s, LANE), lambda b, g, i: (b, g, 0, 0)),
                  pl.BlockSpec((1, 1, s, LANE), lambda b, g, i: (b, N_KV + g, 0, 0)),
                  pl.BlockSpec((1, 1, V_ROWS, s), lambda b, g, i: (b, g, 0, 0)),
                  pl.BlockSpec((1, 1, V_ROWS, s), lambda b, g, i: (b, N_KV + g, 0, 0)),
                  pl.BlockSpec((1, 1, GATE_ROWS, tq), lambda b, g, i: (b, g, 0, i)),
                  pl.BlockSpec((HEAD_DIM, nct), lambda b, g, i: (0, 0)),
                  pl.BlockSpec((s, LANE), lambda b, g, i: (0, 0)),
                  pl.BlockSpec((tq, cols), lambda b, g, i: (0, 0))],
        out_specs=pl.BlockSpec((1, HPG * HEAD_DIM, tq), lambda b, g, i: (b, g, i)),
        out_shape=jax.ShapeDtypeStruct((bsz, ATTN_W, s), BF16),
        scratch_shapes=[pltpu.VMEM((1, cols), F32), pltpu.VMEM((V_ROWS, cols), F32),
                        pltpu.VMEM((1, cols), F32), pltpu.VMEM((V_ROWS, cols), F32),
                        pltpu.VMEM((HEAD_DIM, cols), F32),
                        pltpu.VMEM((tq, cols), F32), pltpu.VMEM((tq, cols), F32),
                        pltpu.VMEM((tq, cols), F32), pltpu.VMEM((tq, cols), F32),
                        pltpu.VMEM((1, cols), F32), pltpu.VMEM((1, cols), F32),
                        pltpu.VMEM((1, cols), F32), pltpu.VMEM((1, cols), F32)],
        compiler_params=_cparams(3),
        name="nsa_attn",
    )(qt, kcn, vct, ksw, ksw, vt, vt, gt, ovt, en, cm)


def _merge_kernel(x_ref, mod_ref, g_ref, uconv_ref, ot_ref, wmg_ref, bmg_ref, wuc_ref, wua_ref, wo_ref, out_ref):
    x = x_ref[0]
    mod = mod_ref[0]
    d = x.shape[-1]
    hb = _modulated_norm(x, g_ref[...], mod[0:1], mod[1:2]).astype(BF16)
    y_conv = _dot(uconv_ref[0], wuc_ref[...])
    y_attn = _dot_tn(ot_ref[0], wua_ref[...])
    g_conv = jax.nn.sigmoid(_dot(hb, wmg_ref[:, 0:d]) + bmg_ref[:, 0:d])
    g_attn = jax.nn.sigmoid(_dot(hb, wmg_ref[:, d:2 * d]) + bmg_ref[:, d:2 * d])
    mixed = _dot((g_conv * y_conv + g_attn * y_attn).astype(BF16), wo_ref[...])
    out_ref[0] = x + mod[2:3] * mixed


def _merge_call(x, mod, g, uconv, ot, l, w_mg, b_mg, w_uc, w_ua, w_o, tm):
    bsz, s, d = x.shape
    return pl.pallas_call(
        _merge_kernel,
        grid=(bsz, s // tm),
        in_specs=[pl.BlockSpec((1, tm, d), lambda b, i: (b, i, 0)),
                  pl.BlockSpec((1, 6, d), lambda b, i: (b, 0, 0)),
                  _whole(1, d),
                  pl.BlockSpec((1, tm, CONV_W), lambda b, i: (b, i, 0)),
                  pl.BlockSpec((1, ATTN_W, tm), lambda b, i: (b, 0, i)),
                  _layer(l, d, 2 * d), _layer(l, 1, 2 * d), _layer(l, CONV_W, d), _layer(l, ATTN_W, d),
                  _layer(l, d, d)],
        out_specs=pl.BlockSpec((1, tm, d), lambda b, i: (b, i, 0)),
        out_shape=jax.ShapeDtypeStruct((bsz, s, d), F32),
        compiler_params=_cparams(2),
        name="merge_out",
    )(x, mod, g, uconv, ot, w_mg, b_mg, w_uc, w_ua, w_o)


def _mlp_kernel(x_ref, mod_ref, g_ref, w1_ref, w2_ref, *rest, tf):
    out_ref = rest[-1]
    x = x_ref[0]
    mod = mod_ref[0]
    hb = _modulated_norm(x, g_ref[...], mod[3:4], mod[4:5]).astype(BF16)
    acc = None
    for c in range(w1_ref.shape[1] // tf):
        a = jnp.maximum(_dot(hb, w1_ref[:, c * tf:(c + 1) * tf]), 0.0)
        part = _dot((a * a).astype(BF16), w2_ref[c * tf:(c + 1) * tf, :])
        acc = part if acc is None else acc + part
    y = x + mod[5:6] * acc
    if len(rest) == 2:
        ms = jnp.mean(y * y, axis=-1, keepdims=True)
        y = y * lax.rsqrt(ms + EPS) * rest[0][...]
    out_ref[0] = y


def _mlp_call(x, mod, g, l, w1, w2, g_final, tm, tf):
    bsz, s, d = x.shape
    d_ff = w1.shape[-1]
    extra = () if g_final is None else (g_final,)
    return pl.pallas_call(
        functools.partial(_mlp_kernel, tf=tf),
        grid=(bsz, s // tm),
        in_specs=[pl.BlockSpec((1, tm, d), lambda b, i: (b, i, 0)),
                  pl.BlockSpec((1, 6, d), lambda b, i: (b, 0, 0)),
                  _whole(1, d), _layer(l, d, d_ff), _layer(l, d_ff, d)] + [_whole(1, d)] * len(extra),
        out_specs=pl.BlockSpec((1, tm, d), lambda b, i: (b, i, 0)),
        out_shape=jax.ShapeDtypeStruct((bsz, s, d), F32),
        compiler_params=_cparams(2),
        name="relu2_mlp",
    )(x, mod, g, w1, w2, *extra)


def _pack_weights(w_in, b_in, cmp_k_w2, cmp_v_w2):
    offs = np.concatenate([[0], np.cumsum(IN_SIZES)])
    col = lambda a, k: a[..., int(offs[k]):int(offs[k + 1])]

    def pad_groups(a):
        lead = a.shape[:-1]
        a = a.reshape(lead + (N_KV, HEAD_DIM))
        a = jnp.pad(a, [(0, 0)] * len(lead) + [(0, 0), (0, LANE - HEAD_DIM)])
        return a.reshape(lead + (N_KV * LANE,))

    def gate_slots(a):
        lead = a.shape[:-1]
        a = a.reshape(lead + (3, N_KV, HPG))
        a = jnp.moveaxis(a, -2, -3)
        a = jnp.pad(a, [(0, 0)] * (len(lead) + 2) + [(0, SUBLANE - HPG)])
        return a.reshape(lead + (N_KV * GATE_ROWS,))

    def main(a):
        return jnp.concatenate([col(a, 0), col(a, 1), col(a, 2), col(a, 4), col(a, 5),
                                pad_groups(col(a, 6)), pad_groups(col(a, 8))], axis=-1)

    def trans(a):
        return jnp.concatenate([col(a, 3), col(a, 7), col(a, 9), gate_slots(col(a, 10))], axis=-1)

    w_main = main(w_in).astype(BF16)
    b_main = main(b_in)[:, None, :]
    w_t = jnp.swapaxes(trans(w_in), 1, 2).astype(BF16)
    b_t = trans(b_in)[:, :, None]
    w_mg = col(w_in, 11).astype(BF16)
    b_mg = col(b_in, 11)[:, None, :]
    w2k = jnp.pad(cmp_k_w2, [(0, 0), (0, 0), (0, LANE - HEAD_DIM)]).astype(BF16)
    w2vt = jnp.swapaxes(cmp_v_w2, 1, 2).astype(BF16)
    return w_main, b_main, w_t, b_t, w_mg, b_mg, w2k, w2vt


def _rope_tables(positions):
    half = HEAD_DIM // 2
    inv_freq = ROPE_THETA ** (-jnp.arange(half, dtype=F32) * 2.0 / HEAD_DIM)
    pos = positions.astype(F32)
    ang = pos[..., None] * jnp.tile(inv_freq, LANE // half)
    cos, sin = jnp.cos(ang), jnp.sin(ang)
    first = (np.arange(LANE) % HEAD_DIM) < half
    lane_tab = jnp.stack([cos, jnp.where(first, -sin, 0.0), jnp.where(first, 0.0, sin)], axis=1)
    ang_t = inv_freq[None, :, None] * pos[:, None, :]
    t_tab = jnp.stack([jnp.cos(ang_t), jnp.sin(ang_t)], axis=1)
    return lane_tab, t_tab


def _static_tables(s, nchunk, tq):
    ns = s // SEL_L
    cstart = np.arange(nchunk) * CMP_D
    jj = np.arange(ns)
    overlap = (cstart[:, None] < (jj[None, :] + 1) * SEL_L) & (cstart[:, None] + CMP_L > jj[None, :] * SEL_L)
    ovt = np.zeros((HEAD_DIM, nchunk), np.float32)
    ovt[:ns] = overlap.T
    en = np.zeros((s, LANE), np.float32)
    en[np.arange(s), HEAD_DIM + np.arange(s) // SEL_L] = 1.0
    cm = np.tile(np.arange(tq)[:, None] <= np.arange(tq)[None, :], (1, HPG)).astype(np.float32)
    return jnp.asarray(ovt, BF16), jnp.asarray(en, BF16), jnp.asarray(cm, BF16)


def kernel(x, c, positions, w_ada, b_ada, g_mix, w_in, b_in, conv_w, cmp_k_w1, cmp_k_w2, cmp_k_pe, cmp_v_w1, cmp_v_w2, cmp_v_pe, w_up_conv, w_up_attn, w_o, g_mlp, w_ff1, w_ff2, g_final):
    bsz, s, d = x.shape
    depth = w_ada.shape[0]
    nchunk = s // CMP_D
    tm = 1024
    tq = 256
    nb = 4 if bsz % 4 == 0 else 1
    assert s % tm == 0 and s % tq == 0 and WINDOW % tq == 0 and nchunk == LANE and N_SEL <= s // SEL_L <= HEAD_DIM

    w_main, b_main, w_t, b_t, w_mg, b_mg, w2k, w2vt = _pack_weights(w_in, b_in, cmp_k_w2, cmp_v_w2)
    w1k, w1v = cmp_k_w1.astype(BF16), cmp_v_w1.astype(BF16)
    pek = jnp.broadcast_to(cmp_k_pe.reshape(depth, 1, CMP_L * HEAD_DIM), (depth, 8, CMP_L * HEAD_DIM))
    pev = jnp.broadcast_to(cmp_v_pe.reshape(depth, 1, CMP_L * HEAD_DIM), (depth, 8, CMP_L * HEAD_DIM))
    w_uc, w_ua, w_ob = w_up_conv.astype(BF16), w_up_attn.astype(BF16), w_o.astype(BF16)
    w_f1, w_f2 = w_ff1.astype(BF16), w_ff2.astype(BF16)
    rope_tab, rope_tab_t = _rope_tables(positions)
    ovt, en, cm = _static_tables(s, nchunk, tq)

    cond = _ada_call(c, w_ada, b_ada).reshape(depth, bsz, 6, d)

    for l in range(depth):
        mod = cond[l]
        uconv, ukv, ksw, qt, vt, gt = _in_call(x, mod, g_mix[l][None], l, w_main, b_main, w_t, b_t,
                                               rope_tab, rope_tab_t, conv_w, tm)
        kcn, vct = _cmp_call(ukv, l, w1k, w1v, pek, pev, w2k, w2vt, nb)
        ot = _attn_call(qt, kcn, vct, ksw, vt, gt, ovt, en, cm, tq)
        x = _merge_call(x, mod, g_mix[l][None], uconv, ot, l, w_mg, b_mg, w_uc, w_ua, w_ob, tm)
        x = _mlp_call(x, mod, g_mlp[l][None], l, w_f1, w_f2, g_final[None] if l == depth - 1 else None, tm, 1024)
    return x
```

```python
import functools

import jax
import jax.numpy as jnp
import numpy as np
from jax import lax
from jax.experimental import pallas as pl
from jax.experimental.pallas import tpu as pltpu

D_MODEL = 1024
CONV_W = 512
CONV_K = 3
N_HEADS = 8
HEAD_DIM = 64
N_KV = 2
HPG = N_HEADS // N_KV
ATTN_W = N_HEADS * HEAD_DIM
KV_W = N_KV * HEAD_DIM
CMP_L = 32
CMP_D = 16
CMP_HIDDEN = 4 * HEAD_DIM
SEL_L = 64
N_SEL = 8
WINDOW = 512
FORCE_BONUS = 1e4
NEG = -1e30
D_FF = 4 * D_MODEL
ROPE_THETA = 10000.0
EPS = 1e-6
IN_SIZES = (CONV_W, CONV_W, CONV_W, ATTN_W, KV_W, KV_W, KV_W, KV_W, KV_W, KV_W, 3 * N_HEADS, 2 * D_MODEL)

LANE = 128
SUBLANE = 8
SEL_BIAS = -1e30
LOG2_E = 1.4426950408889634
VMEM_LIMIT = 56 * 1024 * 1024
GATE_ROWS = 3 * SUBLANE
V_ROWS = HEAD_DIM + 2 * SUBLANE

_C_CB, _C_CC, _C_CU = 0, CONV_W, 2 * CONV_W
_C_KC = 3 * CONV_W
_C_VC = _C_KC + KV_W
_C_KS = _C_VC + KV_W
_C_KW = _C_KS + N_KV * LANE
N_MAIN = _C_KW + N_KV * LANE
_R_Q = 0
_R_VS = ATTN_W
_R_VW = _R_VS + KV_W
_R_G = _R_VW + KV_W
N_TRANS = _R_G + N_KV * GATE_ROWS

F32 = jnp.float32
BF16 = jnp.bfloat16


def _cparams(n_grid):
    return pltpu.CompilerParams(dimension_semantics=("arbitrary",) * n_grid,
                                vmem_limit_bytes=VMEM_LIMIT)


def _whole(*shape):
    return pl.BlockSpec(shape, lambda *_: (0,) * len(shape))


def _layer(l, *shape):
    return pl.BlockSpec((None,) + shape, lambda *_: (l,) + (0,) * len(shape))


def _dot(a, b):
    return jnp.dot(a, b, preferred_element_type=F32)


def _dot_nt(a, b):
    return lax.dot_general(a, b, (((1,), (1,)), ((), ())), preferred_element_type=F32)


def _dot_tn(a, b):
    return lax.dot_general(a, b, (((0,), (0,)), ((), ())), preferred_element_type=F32)


def _modulated_norm(x, g, shift, scale):
    ms = jnp.mean(x * x, axis=-1, keepdims=True)
    y = x * lax.rsqrt(ms + EPS) * g
    return y * (1.0 + scale) + shift


def _ada_kernel(c_ref, w_ref, b_ref, o_ref):
    c = c_ref[...]
    ca = (c * jax.nn.sigmoid(c)).astype(BF16)
    o_ref[0] = _dot(ca, w_ref[0].astype(BF16)) + b_ref[0]


def _ada_call(c, w_ada, b_ada):
    depth, d, n = w_ada.shape
    bsz = c.shape[0]
    tn = 512
    return pl.pallas_call(
        _ada_kernel,
        grid=(depth, n // tn),
        in_specs=[pl.BlockSpec((bsz, d), lambda l, j: (0, 0)),
                  pl.BlockSpec((1, d, tn), lambda l, j: (l, 0, j)),
                  pl.BlockSpec((1, 1, tn), lambda l, j: (l, 0, j))],
        out_specs=pl.BlockSpec((1, bsz, tn), lambda l, j: (l, 0, j)),
        out_shape=jax.ShapeDtypeStruct((depth, bsz, n), F32),
        compiler_params=_cparams(2),
        name="ada_cond",
    )(c, w_ada, b_ada.reshape(depth, 1, n))


def _in_kernel(x_ref, mod_ref, g_ref, w_ref, b_ref, wt_ref, bt_ref, rope_ref, ropet_ref, cw_ref,
               uconv_ref, ukv_ref, ksw_ref, qt_ref, vt_ref, gt_ref, carry_ref, stagek_ref, stagev_ref):
    i = pl.program_id(1)
    x = x_ref[0]
    tm = x.shape[0]
    mod = mod_ref[0]
    hb = _modulated_norm(x, g_ref[...], mod[0:1], mod[1:2]).astype(BF16)

    def seg(a, b):
        return _dot(hb, w_ref[:, a:b]) + b_ref[:, a:b]

    @pl.when(i == 0)
    def _():
        carry_ref[...] = jnp.zeros_like(carry_ref)

    v = seg(_C_CC, _C_CU) * seg(_C_CU, _C_KC)
    prev = carry_ref[...]
    row = lax.broadcasted_iota(jnp.int32, v.shape, 0)
    p1 = jnp.broadcast_to(prev[7:8], v.shape)
    p2 = jnp.broadcast_to(prev[6:7], v.shape)
    v1 = jnp.where(row == 0, p1, pltpu.roll(v, 1, 0))
    v2 = jnp.where(row == 0, p2, jnp.where(row == 1, p1, pltpu.roll(v, 2, 0)))
    cw = cw_ref[...]
    yc = cw[0:1] * v2 + cw[1:2] * v1 + cw[2:3] * v
    carry_ref[...] = v[tm - 8:tm]
    uconv_ref[0] = (seg(_C_CB, _C_CC) * yc).astype(BF16)

    cos2, sa2, sb2 = rope_ref[0, 0], rope_ref[0, 1], rope_ref[0, 2]
    lane = lax.broadcasted_iota(jnp.int32, cos2.shape, 1)
    low = lane < HEAD_DIM
    cos1, sa1, sb1 = jnp.where(low, cos2, 0.0), jnp.where(low, sa2, 0.0), jnp.where(low, sb2, 0.0)

    def rope(z, c, sa, sb):
        return z * c + pltpu.roll(z, LANE - HEAD_DIM // 2, 1) * sa + pltpu.roll(z, HEAD_DIM // 2, 1) * sb

    zkv = seg(_C_KC, _C_KS)
    stagek_ref[...] = rope(zkv[:, 0:KV_W], cos2, sa2, sb2)
    stagev_ref[...] = zkv[:, KV_W:2 * KV_W]
    n_ch = tm // CMP_D
    low_ch = lax.broadcasted_iota(jnp.int32, (n_ch, LANE), 1) < HEAD_DIM
    for a, stage_ref in enumerate((stagek_ref, stagev_ref)):
        for mm in range(CMP_D // 2):
            p0 = stage_ref[pl.ds(2 * mm, n_ch, stride=CMP_D), :]
            p1 = stage_ref[pl.ds(2 * mm + 1, n_ch, stride=CMP_D), :]
            lanes = slice(mm * LANE, (mm + 1) * LANE)
            ukv_ref[0, a, 0, :, lanes] = jnp.where(low_ch, p0, pltpu.roll(p1, HEAD_DIM, 1)).astype(BF16)
            ukv_ref[0, a, 1, :, lanes] = jnp.where(low_ch, pltpu.roll(p0, HEAD_DIM, 1), p1).astype(BF16)
    for br in range(2):
        zk = seg(_C_KS + br * N_KV * LANE, _C_KS + (br + 1) * N_KV * LANE)
        for g in range(N_KV):
            ksw_ref[0, br * N_KV + g] = rope(zk[:, g * LANE:(g + 1) * LANE], cos1, sa1, sb1).astype(BF16)

    zt = _dot_nt(wt_ref[...], hb) + bt_ref[...]
    ct, st = ropet_ref[0, 0], ropet_ref[0, 1]
    half = HEAD_DIM // 2
    scale = HEAD_DIM ** -0.5 * LOG2_E
    for h in range(N_HEADS):
        x1 = zt[_R_Q + h * HEAD_DIM:_R_Q + h * HEAD_DIM + half]
        x2 = zt[_R_Q + h * HEAD_DIM + half:_R_Q + (h + 1) * HEAD_DIM]
        qt_ref[0, h, 0:half, :] = ((x1 * ct - x2 * st) * scale).astype(BF16)
        qt_ref[0, h, half:HEAD_DIM, :] = ((x2 * ct + x1 * st) * scale).astype(BF16)
    for r in range(2 * N_KV):
        vt_ref[0, r, 0:HEAD_DIM, :] = zt[_R_VS + r * HEAD_DIM:_R_VS + (r + 1) * HEAD_DIM].astype(BF16)
        vt_ref[0, r, HEAD_DIM:V_ROWS, :] = jnp.ones((V_ROWS - HEAD_DIM, tm), BF16)
    for gi in range(N_KV):
        gt_ref[0, gi] = jax.nn.sigmoid(zt[_R_G + gi * GATE_ROWS:_R_G + (gi + 1) * GATE_ROWS])


def _in_call(x, mod, g, l, w_main, b_main, w_t, b_t, rope_tab, rope_tab_t, conv_w, tm):
    bsz, s, d = x.shape
    return pl.pallas_call(
        _in_kernel,
        grid=(bsz, s // tm),
        in_specs=[pl.BlockSpec((1, tm, d), lambda b, i: (b, i, 0)),
                  pl.BlockSpec((1, 6, d), lambda b, i: (b, 0, 0)),
                  _whole(1, d),
                  _layer(l, d, N_MAIN),
                  _layer(l, 1, N_MAIN),
                  _layer(l, N_TRANS, d),
                  _layer(l, N_TRANS, 1),
                  pl.BlockSpec((1, 3, tm, LANE), lambda b, i: (b, 0, i, 0)),
                  pl.BlockSpec((1, 2, HEAD_DIM // 2, tm), lambda b, i: (b, 0, 0, i)),
                  _layer(l, CONV_K, CONV_W)],
        out_specs=[pl.BlockSpec((1, tm, CONV_W), lambda b, i: (b, i, 0)),
                   pl.BlockSpec((1, 2, N_KV, tm // CMP_D, CMP_D * HEAD_DIM), lambda b, i: (b, 0, 0, i, 0)),
                   pl.BlockSpec((1, 2 * N_KV, tm, LANE), lambda b, i: (b, 0, i, 0)),
                   pl.BlockSpec((1, N_HEADS, HEAD_DIM, tm), lambda b, i: (b, 0, 0, i)),
                   pl.BlockSpec((1, 2 * N_KV, V_ROWS, tm), lambda b, i: (b, 0, 0, i)),
                   pl.BlockSpec((1, N_KV, GATE_ROWS, tm), lambda b, i: (b, 0, 0, i))],
        out_shape=[jax.ShapeDtypeStruct((bsz, s, CONV_W), BF16),
                   jax.ShapeDtypeStruct((bsz, 2, N_KV, s // CMP_D, CMP_D * HEAD_DIM), BF16),
                   jax.ShapeDtypeStruct((bsz, 2 * N_KV, s, LANE), BF16),
                   jax.ShapeDtypeStruct((bsz, N_HEADS, HEAD_DIM, s), BF16),
                   jax.ShapeDtypeStruct((bsz, 2 * N_KV, V_ROWS, s), BF16),
                   jax.ShapeDtypeStruct((bsz, N_KV, GATE_ROWS, s), F32)],
        scratch_shapes=[pltpu.VMEM((8, CONV_W), F32), pltpu.VMEM((tm, LANE), F32), pltpu.VMEM((tm, LANE), F32)],
        compiler_params=_cparams(2),
        name="in_proj",
    )(x, mod, g, w_main, b_main, w_t, b_t, rope_tab, rope_tab_t, conv_w)


def _cmp_kernel(uk_ref, uv_ref, w1k_ref, w1v_ref, pek_ref, pev_ref, w2k_ref, w2vt_ref,
                kcn_ref, vct_ref, *, nb, nchunk):
    m = nb * N_KV * nchunk
    kh = w1k_ref.shape[0] // 2

    def hidden(u_ref, w1_ref, pe_ref):
        u = jnp.concatenate([u_ref[j, 0, g] for j in range(nb) for g in range(N_KV)], axis=0)
        pe = pe_ref[...]
        pe_hi = pe.astype(BF16)
        pe_lo = (pe - pe_hi.astype(F32)).astype(BF16)
        w1 = w1_ref[...]
        pterm = (_dot(pe_hi, w1) + _dot(pe_lo, w1))[0:1]
        a = _dot(u, w1[0:kh])
        bm = _dot(u, w1[kh:2 * kh])
        hid = a + pltpu.roll(bm, m - 1, 0) + pterm
        return (hid * jax.nn.sigmoid(hid)).astype(BF16)

    hk = hidden(uk_ref, w1k_ref, pek_ref)
    hv = hidden(uv_ref, w1v_ref, pev_ref)
    kc_all = _dot(hk, w2k_ref[...]).astype(BF16)
    for j in range(nb):
        for g in range(N_KV):
            r0 = (j * N_KV + g) * nchunk
            kcn_ref[j, g] = kc_all[r0:r0 + nchunk]
            vct_ref[j, g] = _dot_nt(w2vt_ref[...], hv[r0:r0 + nchunk]).astype(BF16)


def _cmp_call(ukv, l, w1k, w1v, pek, pev, w2k, w2vt, nb):
    bsz, _, _, nchunk, feat = ukv.shape
    return pl.pallas_call(
        functools.partial(_cmp_kernel, nb=nb, nchunk=nchunk),
        grid=(bsz // nb,),
        in_specs=[pl.BlockSpec((nb, 1, N_KV, nchunk, feat), lambda j: (j, 0, 0, 0, 0)),
                  pl.BlockSpec((nb, 1, N_KV, nchunk, feat), lambda j: (j, 1, 0, 0, 0)),
                  _layer(l, 2 * feat, CMP_HIDDEN),
                  _layer(l, 2 * feat, CMP_HIDDEN),
                  _layer(l, 8, 2 * feat),
                  _layer(l, 8, 2 * feat),
                  _layer(l, CMP_HIDDEN, LANE),
                  _layer(l, HEAD_DIM, CMP_HIDDEN)],
        out_specs=[pl.BlockSpec((nb, N_KV, nchunk, LANE), lambda j: (j, 0, 0, 0)),
                   pl.BlockSpec((nb, N_KV, HEAD_DIM, nchunk), lambda j: (j, 0, 0, 0))],
        out_shape=[jax.ShapeDtypeStruct((bsz, N_KV, nchunk, LANE), BF16),
                   jax.ShapeDtypeStruct((bsz, N_KV, HEAD_DIM, nchunk), BF16)],
        compiler_params=_cparams(1),
        name="cmp_mlp",
    )(ukv, ukv, w1k, w1v, pek, pev, w2k, w2vt)


def _attn_kernel(qt_ref, kcn_ref, vct_ref, ks_ref, kw_ref, vst_ref, vwt_ref, gt_ref, ovt_ref, en_ref, cm_ref,
                 o_ref, m_s, acc_s, mw_s, accw_s, ocmp_s,
                 sa_s, sb_s, wa_s, wb_s, mxa_s, mxb_s, mxwa_s, mxwb_s, *, tq, ns, n_sel):
    i = pl.program_id(2)
    cols = HPG * tq
    qt = jnp.concatenate([qt_ref[0, h] for h in range(HPG)], axis=1)
    q_plain = jnp.concatenate([qt, jnp.zeros_like(qt)], axis=0)
    q_in_tile = lax.broadcasted_iota(jnp.int32, (1, cols), 1) & (tq - 1)
    t_col = i * tq + q_in_tile

    krow = lax.broadcasted_iota(jnp.int32, (tq, cols), 0)
    causal = krow <= q_in_tile
    lane_k = lax.broadcasted_iota(jnp.int32, (tq, LANE), 1)

    def score(k_tile, q_aug, buf, mask):
        sc_ref, mx_ref = buf
        sc = _dot(k_tile, q_aug)
        if mask is not None:
            sc = jnp.where(mask, sc, NEG)
        sc_ref[...] = sc
        mx_ref[...] = jnp.max(sc, axis=0, keepdims=True)

    def accumulate(buf, vt_tile, m_r, acc_r):
        sc_ref, mx_ref = buf
        m_old = m_r[...]
        m_new = jnp.maximum(m_old, mx_ref[...])
        alpha = jnp.exp2(m_old - m_new)
        pt = jnp.exp2(sc_ref[...] - m_new)
        acc_r[...] = alpha * acc_r[...] + _dot(vt_tile, pt.astype(BF16))
        m_r[...] = m_new

    def normalized(acc_r):
        acc = acc_r[...]
        return acc[0:HEAD_DIM] * (1.0 / acc[HEAD_DIM:HEAD_DIM + 1])

    for m_r, acc_r in ((m_s, acc_s), (mw_s, accw_s)):
        m_r[...] = jnp.full_like(m_r, NEG)
        acc_r[...] = jnp.zeros_like(acc_r)

    n_back = WINDOW // tq
    brow = lax.broadcasted_iota(jnp.int32, (HEAD_DIM, cols), 0)
    wbias = jnp.where((brow >= 1) & (brow <= n_back) & (brow > i), SEL_BIAS, 0.0).astype(BF16)
    q_win = jnp.concatenate([qt, wbias], axis=0)
    wbufs = ((wa_s, mxwa_s), (wb_s, mxwb_s))

    def win_keys(back):
        k0 = pl.multiple_of(jnp.maximum(i - back, 0) * tq, tq)
        return jnp.where(lane_k == HEAD_DIM + back, 1.0, kw_ref[0, 0, pl.ds(k0, tq), :])

    def win_values(back):
        k0 = pl.multiple_of(jnp.maximum(i - back, 0) * tq, tq)
        return vwt_ref[0, 0, :, pl.ds(k0, tq)]

    def win_edge_score():
        sc = jnp.where(causal, _dot(win_keys(0), q_win), _dot(win_keys(n_back), q_win))
        wa_s[...] = sc
        mxwa_s[...] = jnp.max(sc, axis=0, keepdims=True)

    def win_edge_accumulate():
        m_old = mw_s[...]
        m_new = jnp.maximum(m_old, mxwa_s[...])
        alpha = jnp.exp2(m_old - m_new)
        pt = jnp.exp2(wa_s[...] - m_new).astype(BF16)
        pt_diag = pt * cm_ref[...]
        accw_s[...] = (alpha * accw_s[...] + _dot(win_values(0), pt_diag)
                       + _dot(win_values(n_back), pt - pt_diag))
        mw_s[...] = m_new

    def win_score(back):
        score(win_keys(back), q_win, wbufs[back & 1], None)

    def win_accumulate(back):
        accumulate(wbufs[back & 1], win_values(back), mw_s, accw_s)

    sc0_plain = _dot(ks_ref[0, 0, 0:tq, :], q_plain)

    nct = kcn_ref.shape[2]
    s = _dot(kcn_ref[0, 0], q_plain)
    crow = lax.broadcasted_iota(jnp.int32, (nct, cols), 0)
    cmask = (crow * CMP_D + (CMP_L - 1)) <= t_col
    s = jnp.where(cmask, s, NEG)
    mx = jnp.max(s, axis=0, keepdims=True)
    e = jnp.exp2(s - mx)
    den = jnp.sum(e, axis=0, keepdims=True)
    p = e * jnp.where(mx > 0.5 * NEG, 1.0 / den, 0.0)
    ocmp_s[...] = _dot(vct_ref[0, 0], p.astype(BF16))

    psum = p[:, 0:tq]
    for h in range(1, HPG):
        psum = psum + p[:, h * tq:(h + 1) * tq]
    p_hi = psum.astype(BF16)
    p_lo = (psum - p_hi.astype(F32)).astype(BF16)
    nr = -(-ns // (2 * SUBLANE)) * (2 * SUBLANE)
    imp = (_dot(ovt_ref[...], p_hi) + _dot(ovt_ref[...], p_lo))[0:nr]
    jrow = lax.broadcasted_iota(jnp.int32, (nr, tq), 0)
    jrow_f = jrow.astype(F32)
    bt = (i * tq + lax.broadcasted_iota(jnp.int32, (nr, tq), 1)) // SEL_L
    in_range = jrow < ns
    forced = (jrow == 0) | (jrow == bt) | (jrow == bt - 1)
    val = jnp.where(forced | (jrow > bt), -1.0, imp)
    val = jnp.where(in_range, val, -3.0)
    sel = jnp.where(forced, 1.0, 0.0)
    for _ in range(n_sel - 3):
        top = jnp.max(val, axis=0, keepdims=True)
        first = jnp.min(jnp.where(val == top, jrow_f, 1e3), axis=0, keepdims=True)
        hit = jrow_f == first
        sel = jnp.where(hit, 1.0, sel)
        val = jnp.where(hit, -5.0, val)
    bias_f = jnp.where(in_range & (sel == 0.0), SEL_BIAS, 0.0)
    bias = bias_f.astype(BF16)
    if nr < HEAD_DIM:
        bias = jnp.concatenate([bias, jnp.zeros((HEAD_DIM - nr, tq), BF16)], axis=0)
    q_sel = jnp.concatenate([qt, jnp.concatenate([bias] * HPG, axis=1)], axis=0)

    win_edge_score()
    for back in range(1, n_back):
        win_score(back)
        if back == 1:
            win_edge_accumulate()
        else:
            win_accumulate(back - 1)
    if n_back == 1:
        win_edge_accumulate()
    else:
        win_accumulate(n_back - 1)

    buf_a, buf_b = (sa_s, mxa_s), (sb_s, mxb_s)

    def sel_score(kt, mask, buf):
        k0 = pl.multiple_of(kt * tq, tq)
        kaug = jnp.where(lane_k < HEAD_DIM, ks_ref[0, 0, pl.ds(k0, tq), :], en_ref[pl.ds(k0, tq), :])
        score(kaug, q_sel, buf, mask)

    def sel_accumulate(kt, buf):
        k0 = pl.multiple_of(kt * tq, tq)
        accumulate(buf, vst_ref[0, 0, :, pl.ds(k0, tq)], m_s, acc_s)

    bias_rows = []
    for j in range(tq // SEL_L):
        row = jnp.concatenate([bias_f[j:j + 1]] * HPG, axis=1)
        bias_rows.append(jnp.broadcast_to(row, (SEL_L, cols)))
    sc0 = jnp.where(causal | (i > 0), sc0_plain + jnp.concatenate(bias_rows, axis=0), NEG)
    sa_s[...] = sc0
    mxa_s[...] = jnp.max(sc0, axis=0, keepdims=True)

    def sel_pair(j, carry):
        t = 2 * j
        sel_score(t + 1, None, buf_b)
        sel_accumulate(t, buf_a)
        sel_score(t + 2, None, buf_a)
        sel_accumulate(t + 1, buf_b)
        return carry

    def sel_quad(j, carry):
        sel_pair(2 * j, carry)
        sel_pair(2 * j + 1, carry)
        return carry

    n_pairs = jnp.maximum(i - 1, 0) // 2
    lax.fori_loop(0, n_pairs // 2, sel_quad, 0)

    @pl.when((n_pairs & 1) == 1)
    def _():
        sel_pair(n_pairs - 1, 0)

    @pl.when(i == 0)
    def _():
        sel_accumulate(0, buf_a)

    @pl.when((i & 1) == 1)
    def _():
        sel_score(i, causal, buf_b)
        sel_accumulate(i - 1, buf_a)
        sel_accumulate(i, buf_b)

    @pl.when(((i & 1) == 0) & (i > 0))
    def _():
        sel_score(i - 1, None, buf_b)
        sel_accumulate(i - 2, buf_a)
        sel_score(i, causal, buf_a)
        sel_accumulate(i - 1, buf_b)
        sel_accumulate(i, buf_a)

    gt = gt_ref[0, 0]
    o_cmp = ocmp_s[...]
    o_sel = normalized(acc_s)
    o_win = normalized(accw_s)
    for h in range(HPG):
        c0, c1 = h * tq, (h + 1) * tq
        o_h = (gt[h:h + 1] * o_cmp[:, c0:c1] + gt[SUBLANE + h:SUBLANE + h + 1] * o_sel[:, c0:c1]
               + gt[2 * SUBLANE + h:2 * SUBLANE + h + 1] * o_win[:, c0:c1])
        o_ref[0, h * HEAD_DIM:(h + 1) * HEAD_DIM, :] = o_h.astype(BF16)


def _attn_call(qt, kcn, vct, ksw, vt, gt, ovt, en, cm, tq):
    bsz, _, _, s = qt.shape
    ns = s // SEL_L
    cols = HPG * tq
    nct = kcn.shape[2]
    return pl.pallas_call(
        functools.partial(_attn_kernel, tq=tq, ns=ns, n_sel=min(N_SEL, ns)),
        grid=(bsz, N_KV, s // tq),
        in_specs=[pl.BlockSpec((1, HPG, HEAD_DIM, tq), lambda b, g, i: (b, g, 0, i)),
                  pl.BlockSpec((1, 1, nct, LANE), lambda b, g, i: (b, g, 0, 0)),
                  pl.BlockSpec((1, 1, HEAD_DIM, nct), lambda b, g, i: (b, g, 0, 0)),
                  pl.BlockSpec((1, 1, s, LANE), lambda b, g, i: (b, g, 0, 0)),
                  pl.BlockSpec((1, 1, s, LANE), lambda b, g, i: (b, N_KV + g, 0, 0)),
                  pl.BlockSpec((1, 1, V_ROWS, s), lambda b, g, i: (b, g, 0, 0)),
                  pl.BlockSpec((1, 1, V_ROWS, s), lambda b, g, i: (b, N_KV + g, 0, 0)),
                  pl.BlockSpec((1, 1, GATE_ROWS, tq), lambda b, g, i: (b, g, 0, i)),
                  pl.BlockSpec((HEAD_DIM, nct), lambda b, g, i: (0, 0)),
                  pl.BlockSpec((s, LANE), lambda b, g, i: (0, 0)),
                  pl.BlockSpec((tq, cols), lambda b, g, i: (0, 0))],
        out_specs=pl.BlockSpec((1, HPG * HEAD_DIM, tq), lambda b, g, i: (b, g, i)),
        out_shape=jax.ShapeDtypeStruct((bsz, ATTN_W, s), BF16),
        scratch_shapes=[pltpu.VMEM((1, cols), F32), pltpu.VMEM((V_ROWS, cols), F32),
                        pltpu.VMEM((1, cols), F32), pltpu.VMEM((V_ROWS, cols), F32),
                        pltpu.VMEM((HEAD_DIM, cols), F32),
                        pltpu.VMEM((tq, cols), F32), pltpu.VMEM((tq, cols), F32),
                        pltpu.VMEM((tq, cols), F32), pltpu.VMEM((tq, cols), F32),
                        pltpu.VMEM((1, cols), F32), pltpu.VMEM((1, cols), F32),
                        pltpu.VMEM((1, cols), F32), pltpu.VMEM((1, cols), F32)],
        compiler_params=_cparams(3),
        name="nsa_attn",
    )(qt, kcn, vct, ksw, ksw, vt, vt, gt, ovt, en, cm)


def _merge_kernel(x_ref, mod_ref, g_ref, uconv_ref, ot_ref, wmg_ref, bmg_ref, wuc_ref, wua_ref, wo_ref, out_ref):
    x = x_ref[0]
    mod = mod_ref[0]
    d = x.shape[-1]
    hb = _modulated_norm(x, g_ref[...], mod[0:1], mod[1:2]).astype(BF16)
    y_conv = _dot(uconv_ref[0], wuc_ref[...])
    y_attn = _dot_tn(ot_ref[0], wua_ref[...])
    g_conv = jax.nn.sigmoid(_dot(hb, wmg_ref[:, 0:d]) + bmg_ref[:, 0:d])
    g_attn = jax.nn.sigmoid(_dot(hb, wmg_ref[:, d:2 * d]) + bmg_ref[:, d:2 * d])
    mixed = _dot((g_conv * y_conv + g_attn * y_attn).astype(BF16), wo_ref[...])
    out_ref[0] = x + mod[2:3] * mixed


def _merge_call(x, mod, g, uconv, ot, l, w_mg, b_mg, w_uc, w_ua, w_o, tm):
    bsz, s, d = x.shape
    return pl.pallas_call(
        _merge_kernel,
        grid=(bsz, s // tm),
        in_specs=[pl.BlockSpec((1, tm, d), lambda b, i: (b, i, 0)),
                  pl.BlockSpec((1, 6, d), lambda b, i: (b, 0, 0)),
                  _whole(1, d),
                  pl.BlockSpec((1, tm, CONV_W), lambda b, i: (b, i, 0)),
                  pl.BlockSpec((1, ATTN_W, tm), lambda b, i: (b, 0, i)),
                  _layer(l, d, 2 * d), _layer(l, 1, 2 * d), _layer(l, CONV_W, d), _layer(l, ATTN_W, d),
                  _layer(l, d, d)],
        out_specs=pl.BlockSpec((1, tm, d), lambda b, i: (b, i, 0)),
        out_shape=jax.ShapeDtypeStruct((bsz, s, d), F32),
        compiler_params=_cparams(2),
        name="merge_out",
    )(x, mod, g, uconv, ot, w_mg, b_mg, w_uc, w_ua, w_o)


def _mlp_kernel(x_ref, mod_ref, g_ref, w1_ref, w2_ref, *rest, tf):
    out_ref = rest[-1]
    x = x_ref[0]
    mod = mod_ref[0]
    hb = _modulated_norm(x, g_ref[...], mod[3:4], mod[4:5]).astype(BF16)
    acc = None
    for c in range(w1_ref.shape[1] // tf):
        a = jnp.maximum(_dot(hb, w1_ref[:, c * tf:(c + 1) * tf]), 0.0)
        part = _dot((a * a).astype(BF16), w2_ref[c * tf:(c + 1) * tf, :])
        acc = part if acc is None else acc + part
    y = x + mod[5:6] * acc
    if len(rest) == 2:
        ms = jnp.mean(y * y, axis=-1, keepdims=True)
        y = y * lax.rsqrt(ms + EPS) * rest[0][...]
    out_ref[0] = y


def _mlp_call(x, mod, g, l, w1, w2, g_final, tm, tf):
    bsz, s, d = x.shape
    d_ff = w1.shape[-1]
    extra = () if g_final is None else (g_final,)
    return pl.pallas_call(
        functools.partial(_mlp_kernel, tf=tf),
        grid=(bsz, s // tm),
        in_specs=[pl.BlockSpec((1, tm, d), lambda b, i: (b, i, 0)),
                  pl.BlockSpec((1, 6, d), lambda b, i: (b, 0, 0)),
                  _whole(1, d), _layer(l, d, d_ff), _layer(l, d_ff, d)] + [_whole(1, d)] * len(extra),
        out_specs=pl.BlockSpec((1, tm, d), lambda b, i: (b, i, 0)),
        out_shape=jax.ShapeDtypeStruct((bsz, s, d), F32),
        compiler_params=_cparams(2),
        name="relu2_mlp",
    )(x, mod, g, w1, w2, *extra)


def _pack_weights(w_in, b_in, cmp_k_w2, cmp_v_w2):
    offs = np.concatenate([[0], np.cumsum(IN_SIZES)])
    col = lambda a, k: a[..., int(offs[k]):int(offs[k + 1])]

    def pad_groups(a):
        lead = a.shape[:-1]
        a = a.reshape(lead + (N_KV, HEAD_DIM))
        a = jnp.pad(a, [(0, 0)] * len(lead) + [(0, 0), (0, LANE - HEAD_DIM)])
        return a.reshape(lead + (N_KV * LANE,))

    def gate_slots(a):
        lead = a.shape[:-1]
        a = a.reshape(lead + (3, N_KV, HPG))
        a = jnp.moveaxis(a, -2, -3)
        a = jnp.pad(a, [(0, 0)] * (len(lead) + 2) + [(0, SUBLANE - HPG)])
        return a.reshape(lead + (N_KV * GATE_ROWS,))

    def main(a):
        return jnp.concatenate([col(a, 0), col(a, 1), col(a, 2), col(a, 4), col(a, 5),
                                pad_groups(col(a, 6)), pad_groups(col(a, 8))], axis=-1)

    def trans(a):
        return jnp.concatenate([col(a, 3), col(a, 7), col(a, 9), gate_slots(col(a, 10))], axis=-1)

    w_main = main(w_in).astype(BF16)
    b_main = main(b_in)[:, None, :]
    w_t = jnp.swapaxes(trans(w_in), 1, 2).astype(BF16)
    b_t = trans(b_in)[:, :, None]
    w_mg = col(w_in, 11).astype(BF16)
    b_mg = col(b_in, 11)[:, None, :]
    w2k = jnp.pad(cmp_k_w2, [(0, 0), (0, 0), (0, LANE - HEAD_DIM)]).astype(BF16)
    w2vt = jnp.swapaxes(cmp_v_w2, 1, 2).astype(BF16)
    return w_main, b_main, w_t, b_t, w_mg, b_mg, w2k, w2vt


def _rope_tables(positions):
    half = HEAD_DIM // 2
    inv_freq = ROPE_THETA ** (-jnp.arange(half, dtype=F32) * 2.0 / HEAD_DIM)
    pos = positions.astype(F32)
    ang = pos[..., None] * jnp.tile(inv_freq, LANE // half)
    cos, sin = jnp.cos(ang), jnp.sin(ang)
    first = (np.arange(LANE) % HEAD_DIM) < half
    lane_tab = jnp.stack([cos, jnp.where(first, -sin, 0.0), jnp.where(first, 0.0, sin)], axis=1)
    ang_t = inv_freq[None, :, None] * pos[:, None, :]
    t_tab = jnp.stack([jnp.cos(ang_t), jnp.sin(ang_t)], axis=1)
    return lane_tab, t_tab


def _static_tables(s, nchunk, tq):
    ns = s // SEL_L
    cstart = np.arange(nchunk) * CMP_D
    jj = np.arange(ns)
    overlap = (cstart[:, None] < (jj[None, :] + 1) * SEL_L) & (cstart[:, None] + CMP_L > jj[None, :] * SEL_L)
    ovt = np.zeros((HEAD_DIM, nchunk), np.float32)
    ovt[:ns] = overlap.T
    en = np.zeros((s, LANE), np.float32)
    en[np.arange(s), HEAD_DIM + np.arange(s) // SEL_L] = 1.0
    cm = np.tile(np.arange(tq)[:, None] <= np.arange(tq)[None, :], (1, HPG)).astype(np.float32)
    return jnp.asarray(ovt, BF16), jnp.asarray(en, BF16), jnp.asarray(cm, BF16)


def kernel(x, c, positions, w_ada, b_ada, g_mix, w_in, b_in, conv_w, cmp_k_w1, cmp_k_w2, cmp_k_pe, cmp_v_w1, cmp_v_w2, cmp_v_pe, w_up_conv, w_up_attn, w_o, g_mlp, w_ff1, w_ff2, g_final):
    bsz, s, d = x.shape
    depth = w_ada.shape[0]
    nchunk = s // CMP_D
    tm = 1024
    tq = 256
    nb = 4 if bsz % 4 == 0 else 1
    assert s % tm == 0 and s % tq == 0 and WINDOW % tq == 0 and nchunk == LANE and N_SEL <= s // SEL_L <= HEAD_DIM

    w_main, b_main, w_t, b_t, w_mg, b_mg, w2k, w2vt = _pack_weights(w_in, b_in, cmp_k_w2, cmp_v_w2)
    w1k, w1v = cmp_k_w1.astype(BF16), cmp_v_w1.astype(BF16)
    pek = jnp.broadcast_to(cmp_k_pe.reshape(depth, 1, CMP_L * HEAD_DIM), (depth, 8, CMP_L * HEAD_DIM))
    pev = jnp.broadcast_to(cmp_v_pe.reshape(depth, 1, CMP_L * HEAD_DIM), (depth, 8, CMP_L * HEAD_DIM))
    w_uc, w_ua, w_ob = w_up_conv.astype(BF16), w_up_attn.astype(BF16), w_o.astype(BF16)
    w_f1, w_f2 = w_ff1.astype(BF16), w_ff2.astype(BF16)
    rope_tab, rope_tab_t = _rope_tables(positions)
    ovt, en, cm = _static_tables(s, nchunk, tq)

    cond = _ada_call(c, w_ada, b_ada).reshape(depth, bsz, 6, d)

    for l in range(depth):
        mod = cond[l]
        uconv, ukv, ksw, qt, vt, gt = _in_call(x, mod, g_mix[l][None], l, w_main, b_main, w_t, b_t,
                                               rope_tab, rope_tab_t, conv_w, tm)
        kcn, vct = _cmp_call(ukv, l, w1k, w1v, pek, pev, w2k, w2vt, nb)
        ot = _attn_call(qt, kcn, vct, ksw, vt, gt, ovt, en, cm, tq)
        x = _merge_call(x, mod, g_mix[l][None], uconv, ot, l, w_mg, b_mg, w_uc, w_ua, w_ob, tm)
        x = _mlp_call(x, mod, g_mlp[l][None], l, w_f1, w_f2, g_final[None] if l == depth - 1 else None, tm, 1024)
    return x
```

```python
import functools

import jax
import jax.numpy as jnp
import numpy as np
from jax import lax
from jax.experimental import pallas as pl
from jax.experimental.pallas import tpu as pltpu

D_MODEL = 1024
CONV_W = 512
CONV_K = 3
N_HEADS = 8
HEAD_DIM = 64
N_KV = 2
HPG = N_HEADS // N_KV
ATTN_W = N_HEADS * HEAD_DIM
KV_W = N_KV * HEAD_DIM
CMP_L = 32
CMP_D = 16
CMP_HIDDEN = 4 * HEAD_DIM
SEL_L = 64
N_SEL = 8
WINDOW = 512
FORCE_BONUS = 1e4
NEG = -1e30
D_FF = 4 * D_MODEL
ROPE_THETA = 10000.0
EPS = 1e-6
IN_SIZES = (CONV_W, CONV_W, CONV_W, ATTN_W, KV_W, KV_W, KV_W, KV_W, KV_W, KV_W, 3 * N_HEADS, 2 * D_MODEL)

LANE = 128
SUBLANE = 8
SEL_BIAS = -1e30
LOG2_E = 1.4426950408889634
VMEM_LIMIT = 56 * 1024 * 1024
GATE_ROWS = 3 * SUBLANE
V_ROWS = HEAD_DIM + 2 * SUBLANE

_C_CB, _C_CC, _C_CU = 0, CONV_W, 2 * CONV_W
_C_KC = 3 * CONV_W
_C_VC = _C_KC + KV_W
_C_KS = _C_VC + KV_W
_C_KW = _C_KS + N_KV * LANE
N_MAIN = _C_KW + N_KV * LANE
_R_Q = 0
_R_VS = ATTN_W
_R_VW = _R_VS + KV_W
_R_G = _R_VW + KV_W
N_TRANS = _R_G + N_KV * GATE_ROWS

F32 = jnp.float32
BF16 = jnp.bfloat16


def _cparams(n_grid):
    return pltpu.CompilerParams(dimension_semantics=("arbitrary",) * n_grid,
                                vmem_limit_bytes=VMEM_LIMIT)


def _whole(*shape):
    return pl.BlockSpec(shape, lambda *_: (0,) * len(shape))


def _layer(l, *shape):
    return pl.BlockSpec((None,) + shape, lambda *_: (l,) + (0,) * len(shape))


def _dot(a, b):
    return jnp.dot(a, b, preferred_element_type=F32)


def _dot_nt(a, b):
    return lax.dot_general(a, b, (((1,), (1,)), ((), ())), preferred_element_type=F32)


def _dot_tn(a, b):
    return lax.dot_general(a, b, (((0,), (0,)), ((), ())), preferred_element_type=F32)


def _modulated_norm(x, g, shift, scale):
    ms = jnp.mean(x * x, axis=-1, keepdims=True)
    y = x * lax.rsqrt(ms + EPS) * g
    return y * (1.0 + scale) + shift


def _ada_kernel(c_ref, w_ref, b_ref, o_ref):
    c = c_ref[...]
    ca = (c * jax.nn.sigmoid(c)).astype(BF16)
    o_ref[0] = _dot(ca, w_ref[0].astype(BF16)) + b_ref[0]


def _ada_call(c, w_ada, b_ada):
    depth, d, n = w_ada.shape
    bsz = c.shape[0]
    tn = 512
    return pl.pallas_call(
        _ada_kernel,
        grid=(depth, n // tn),
        in_specs=[pl.BlockSpec((bsz, d), lambda l, j: (0, 0)),
                  pl.BlockSpec((1, d, tn), lambda l, j: (l, 0, j)),
                  pl.BlockSpec((1, 1, tn), lambda l, j: (l, 0, j))],
        out_specs=pl.BlockSpec((1, bsz, tn), lambda l, j: (l, 0, j)),
        out_shape=jax.ShapeDtypeStruct((depth, bsz, n), F32),
        compiler_params=_cparams(2),
        name="ada_cond",
    )(c, w_ada, b_ada.reshape(depth, 1, n))


def _in_kernel(x_ref, mod_ref, g_ref, w_ref, b_ref, wt_ref, bt_ref, rope_ref, ropet_ref, cw_ref,
               uconv_ref, ukv_ref, ksw_ref, qt_ref, vt_ref, gt_ref, carry_ref, stagek_ref, stagev_ref):
    i = pl.program_id(1)
    x = x_ref[0]
    tm = x.shape[0]
    mod = mod_ref[0]
    hb = _modulated_norm(x, g_ref[...], mod[0:1], mod[1:2]).astype(BF16)

    def seg(a, b):
        return _dot(hb, w_ref[:, a:b]) + b_ref[:, a:b]

    @pl.when(i == 0)
    def _():
        carry_ref[...] = jnp.zeros_like(carry_ref)

    v = seg(_C_CC, _C_CU) * seg(_C_CU, _C_KC)
    prev = carry_ref[...]
    row = lax.broadcasted_iota(jnp.int32, v.shape, 0)
    p1 = jnp.broadcast_to(prev[7:8], v.shape)
    p2 = jnp.broadcast_to(prev[6:7], v.shape)
    v1 = jnp.where(row == 0, p1, pltpu.roll(v, 1, 0))
    v2 = jnp.where(row == 0, p2, jnp.where(row == 1, p1, pltpu.roll(v, 2, 0)))
    cw = cw_ref[...]
    yc = cw[0:1] * v2 + cw[1:2] * v1 + cw[2:3] * v
    carry_ref[...] = v[tm - 8:tm]
    uconv_ref[0] = (seg(_C_CB, _C_CC) * yc).astype(BF16)

    cos2, sa2, sb2 = rope_ref[0, 0], rope_ref[0, 1], rope_ref[0, 2]
    lane = lax.broadcasted_iota(jnp.int32, cos2.shape, 1)
    low = lane < HEAD_DIM
    cos1, sa1, sb1 = jnp.where(low, cos2, 0.0), jnp.where(low, sa2, 0.0), jnp.where(low, sb2, 0.0)

    def rope(z, c, sa, sb):
        return z * c + pltpu.roll(z, LANE - HEAD_DIM // 2, 1) * sa + pltpu.roll(z, HEAD_DIM // 2, 1) * sb

    zkv = seg(_C_KC, _C_KS)
    stagek_ref[...] = rope(zkv[:, 0:KV_W], cos2, sa2, sb2)
    stagev_ref[...] = zkv[:, KV_W:2 * KV_W]
    n_ch = tm // CMP_D
    low_ch = lax.broadcasted_iota(jnp.int32, (n_ch, LANE), 1) < HEAD_DIM
    for a, stage_ref in enumerate((stagek_ref, stagev_ref)):
        for mm in range(CMP_D // 2):
            p0 = stage_ref[pl.ds(2 * mm, n_ch, stride=CMP_D), :]
            p1 = stage_ref[pl.ds(2 * mm + 1, n_ch, stride=CMP_D), :]
            lanes = slice(mm * LANE, (mm + 1) * LANE)
            ukv_ref[0, a, 0, :, lanes] = jnp.where(low_ch, p0, pltpu.roll(p1, HEAD_DIM, 1)).astype(BF16)
            ukv_ref[0, a, 1, :, lanes] = jnp.where(low_ch, pltpu.roll(p0, HEAD_DIM, 1), p1).astype(BF16)
    for br in range(2):
        zk = seg(_C_KS + br * N_KV * LANE, _C_KS + (br + 1) * N_KV * LANE)
        for g in range(N_KV):
            ksw_ref[0, br * N_KV + g] = rope(zk[:, g * LANE:(g + 1) * LANE], cos1, sa1, sb1).astype(BF16)

    zt = _dot_nt(wt_ref[...], hb) + bt_ref[...]
    ct, st = ropet_ref[0, 0], ropet_ref[0, 1]
    half = HEAD_DIM // 2
    scale = HEAD_DIM ** -0.5 * LOG2_E
    for h in range(N_HEADS):
        x1 = zt[_R_Q + h * HEAD_DIM:_R_Q + h * HEAD_DIM + half]
        x2 = zt[_R_Q + h * HEAD_DIM + half:_R_Q + (h + 1) * HEAD_DIM]
        qt_ref[0, h, 0:half, :] = ((x1 * ct - x2 * st) * scale).astype(BF16)
        qt_ref[0, h, half:HEAD_DIM, :] = ((x2 * ct + x1 * st) * scale).astype(BF16)
    for r in range(2 * N_KV):
        vt_ref[0, r, 0:HEAD_DIM, :] = zt[_R_VS + r * HEAD_DIM:_R_VS + (r + 1) * HEAD_DIM].astype(BF16)
        vt_ref[0, r, HEAD_DIM:V_ROWS, :] = jnp.ones((V_ROWS - HEAD_DIM, tm), BF16)
    for gi in range(N_KV):
        gt_ref[0, gi] = jax.nn.sigmoid(zt[_R_G + gi * GATE_ROWS:_R_G + (gi + 1) * GATE_ROWS])


def _in_call(x, mod, g, l, w_main, b_main, w_t, b_t, rope_tab, rope_tab_t, conv_w, tm):
    bsz, s, d = x.shape
    return pl.pallas_call(
        _in_kernel,
        grid=(bsz, s // tm),
        in_specs=[pl.BlockSpec((1, tm, d), lambda b, i: (b, i, 0)),
                  pl.BlockSpec((1, 6, d), lambda b, i: (b, 0, 0)),
                  _whole(1, d),
                  _layer(l, d, N_MAIN),
                  _layer(l, 1, N_MAIN),
                  _layer(l, N_TRANS, d),
                  _layer(l, N_TRANS, 1),
                  pl.BlockSpec((1, 3, tm, LANE), lambda b, i: (b, 0, i, 0)),
                  pl.BlockSpec((1, 2, HEAD_DIM // 2, tm), lambda b, i: (b, 0, 0, i)),
                  _layer(l, CONV_K, CONV_W)],
        out_specs=[pl.BlockSpec((1, tm, CONV_W), lambda b, i: (b, i, 0)),
                   pl.BlockSpec((1, 2, N_KV, tm // CMP_D, CMP_D * HEAD_DIM), lambda b, i: (b, 0, 0, i, 0)),
                   pl.BlockSpec((1, 2 * N_KV, tm, LANE), lambda b, i: (b, 0, i, 0)),
                   pl.BlockSpec((1, N_HEADS, HEAD_DIM, tm), lambda b, i: (b, 0, 0, i)),
                   pl.BlockSpec((1, 2 * N_KV, V_ROWS, tm), lambda b, i: (b, 0, 0, i)),
                   pl.BlockSpec((1, N_KV, GATE_ROWS, tm), lambda b, i: (b, 0, 0, i))],
        out_shape=[jax.ShapeDtypeStruct((bsz, s, CONV_W), BF16),
                   jax.ShapeDtypeStruct((bsz, 2, N_KV, s // CMP_D, CMP_D * HEAD_DIM), BF16),
                   jax.ShapeDtypeStruct((bsz, 2 * N_KV, s, LANE), BF16),
                   jax.ShapeDtypeStruct((bsz, N_HEADS, HEAD_DIM, s), BF16),
                   jax.ShapeDtypeStruct((bsz, 2 * N_KV, V_ROWS, s), BF16),
                   jax.ShapeDtypeStruct((bsz, N_KV, GATE_ROWS, s), F32)],
        scratch_shapes=[pltpu.VMEM((8, CONV_W), F32), pltpu.VMEM((tm, LANE), F32), pltpu.VMEM((tm, LANE), F32)],
        compiler_params=_cparams(2),
        name="in_proj",
    )(x, mod, g, w_main, b_main, w_t, b_t, rope_tab, rope_tab_t, conv_w)


def _cmp_kernel(uk_ref, uv_ref, w1k_ref, w1v_ref, pek_ref, pev_ref, w2k_ref, w2vt_ref,
                kcn_ref, vct_ref, *, nb, nchunk):
    m = nb * N_KV * nchunk
    kh = w1k_ref.shape[0] // 2

    def hidden(u_ref, w1_ref, pe_ref):
        u = jnp.concatenate([u_ref[j, 0, g] for j in range(nb) for g in range(N_KV)], axis=0)
        pe = pe_ref[...]
        pe_hi = pe.astype(BF16)
        pe_lo = (pe - pe_hi.astype(F32)).astype(BF16)
        w1 = w1_ref[...]
        pterm = (_dot(pe_hi, w1) + _dot(pe_lo, w1))[0:1]
        a = _dot(u, w1[0:kh])
        bm = _dot(u, w1[kh:2 * kh])
        hid = a + pltpu.roll(bm, m - 1, 0) + pterm
        return (hid * jax.nn.sigmoid(hid)).astype(BF16)

    hk = hidden(uk_ref, w1k_ref, pek_ref)
    hv = hidden(uv_ref, w1v_ref, pev_ref)
    kc_all = _dot(hk, w2k_ref[...]).astype(BF16)
    for j in range(nb):
        for g in range(N_KV):
            r0 = (j * N_KV + g) * nchunk
            kcn_ref[j, g] = kc_all[r0:r0 + nchunk]
            vct_ref[j, g] = _dot_nt(w2vt_ref[...], hv[r0:r0 + nchunk]).astype(BF16)


def _cmp_call(ukv, l, w1k, w1v, pek, pev, w2k, w2vt, nb):
    bsz, _, _, nchunk, feat = ukv.shape
    return pl.pallas_call(
        functools.partial(_cmp_kernel, nb=nb, nchunk=nchunk),
        grid=(bsz // nb,),
        in_specs=[pl.BlockSpec((nb, 1, N_KV, nchunk, feat), lambda j: (j, 0, 0, 0, 0)),
                  pl.BlockSpec((nb, 1, N_KV, nchunk, feat), lambda j: (j, 1, 0, 0, 0)),
                  _layer(l, 2 * feat, CMP_HIDDEN),
                  _layer(l, 2 * feat, CMP_HIDDEN),
                  _layer(l, 8, 2 * feat),
                  _layer(l, 8, 2 * feat),
                  _layer(l, CMP_HIDDEN, LANE),
                  _layer(l, HEAD_DIM, CMP_HIDDEN)],
        out_specs=[pl.BlockSpec((nb, N_KV, nchunk, LANE), lambda j: (j, 0, 0, 0)),
                   pl.BlockSpec((nb, N_KV, HEAD_DIM, nchunk), lambda j: (j, 0, 0, 0))],
        out_shape=[jax.ShapeDtypeStruct((bsz, N_KV, nchunk, LANE), BF16),
                   jax.ShapeDtypeStruct((bsz, N_KV, HEAD_DIM, nchunk), BF16)],
        compiler_params=_cparams(1),
        name="cmp_mlp",
    )(ukv, ukv, w1k, w1v, pek, pev, w2k, w2vt)


def _attn_kernel(qt_ref, kcn_ref, vct_ref, ks_ref, kw_ref, vst_ref, vwt_ref, gt_ref, ovt_ref, en_ref, cm_ref,
                 o_ref, m_s, acc_s, mw_s, accw_s, ocmp_s,
                 sa_s, sb_s, wa_s, wb_s, mxa_s, mxb_s, mxwa_s, mxwb_s, *, tq, ns, n_sel):
    i = pl.program_id(2)
    cols = HPG * tq
    qt = jnp.concatenate([qt_ref[0, h] for h in range(HPG)], axis=1)
    q_plain = jnp.concatenate([qt, jnp.zeros_like(qt)], axis=0)
    q_in_tile = lax.broadcasted_iota(jnp.int32, (1, cols), 1) & (tq - 1)
    t_col = i * tq + q_in_tile

    krow = lax.broadcasted_iota(jnp.int32, (tq, cols), 0)
    causal = krow <= q_in_tile
    lane_k = lax.broadcasted_iota(jnp.int32, (tq, LANE), 1)

    def score(k_tile, q_aug, buf, mask):
        sc_ref, mx_ref = buf
        sc = _dot(k_tile, q_aug)
        if mask is not None:
            sc = jnp.where(mask, sc, NEG)
        sc_ref[...] = sc
        mx_ref[...] = jnp.max(sc, axis=0, keepdims=True)

    def accumulate(buf, vt_tile, m_r, acc_r):
        sc_ref, mx_ref = buf
        m_old = m_r[...]
        m_new = jnp.maximum(m_old, mx_ref[...])
        alpha = jnp.exp2(m_old - m_new)
        pt = jnp.exp2(sc_ref[...] - m_new)
        acc_r[...] = alpha * acc_r[...] + _dot(vt_tile, pt.astype(BF16))
        m_r[...] = m_new

    def normalized(acc_r):
        acc = acc_r[...]
        return acc[0:HEAD_DIM] * (1.0 / acc[HEAD_DIM:HEAD_DIM + 1])

    for m_r, acc_r in ((m_s, acc_s), (mw_s, accw_s)):
        m_r[...] = jnp.full_like(m_r, NEG)
        acc_r[...] = jnp.zeros_like(acc_r)

    n_back = WINDOW // tq
    brow = lax.broadcasted_iota(jnp.int32, (HEAD_DIM, cols), 0)
    wbias = jnp.where((brow >= 1) & (brow <= n_back) & (brow > i), SEL_BIAS, 0.0).astype(BF16)
    q_win = jnp.concatenate([qt, wbias], axis=0)
    wbufs = ((wa_s, mxwa_s), (wb_s, mxwb_s))

    def win_keys(back):
        k0 = pl.multiple_of(jnp.maximum(i - back, 0) * tq, tq)
        return jnp.where(lane_k == HEAD_DIM + back, 1.0, kw_ref[0, 0, pl.ds(k0, tq), :])

    def win_values(back):
        k0 = pl.multiple_of(jnp.maximum(i - back, 0) * tq, tq)
        return vwt_ref[0, 0, :, pl.ds(k0, tq)]

    def win_edge_score():
        sc = jnp.where(causal, _dot(win_keys(0), q_win), _dot(win_keys(n_back), q_win))
        wa_s[...] = sc
        mxwa_s[...] = jnp.max(sc, axis=0, keepdims=True)

    def win_edge_accumulate():
        m_old = mw_s[...]
        m_new = jnp.maximum(m_old, mxwa_s[...])
        alpha = jnp.exp2(m_old - m_new)
        pt = jnp.exp2(wa_s[...] - m_new).astype(BF16)
        pt_diag = pt * cm_ref[...]
        accw_s[...] = (alpha * accw_s[...] + _dot(win_values(0), pt_diag)
                       + _dot(win_values(n_back), pt - pt_diag))
        mw_s[...] = m_new

    def win_score(back):
        score(win_keys(back), q_win, wbufs[back & 1], None)

    def win_accumulate(back):
        accumulate(wbufs[back & 1], win_values(back), mw_s, accw_s)

    win_edge_score()

    nct = kcn_ref.shape[2]
    s = _dot(kcn_ref[0, 0], q_plain)
    crow = lax.broadcasted_iota(jnp.int32, (nct, cols), 0)
    cmask = (crow * CMP_D + (CMP_L - 1)) <= t_col
    s = jnp.where(cmask, s, NEG)
    mx = jnp.max(s, axis=0, keepdims=True)
    e = jnp.exp2(s - mx)
    den = jnp.sum(e, axis=0, keepdims=True)
    p = e * jnp.where(mx > 0.5 * NEG, 1.0 / den, 0.0)
    ocmp_s[...] = _dot(vct_ref[0, 0], p.astype(BF16))

    psum = p[:, 0:tq]
    for h in range(1, HPG):
        psum = psum + p[:, h * tq:(h + 1) * tq]
    p_hi = psum.astype(BF16)
    p_lo = (psum - p_hi.astype(F32)).astype(BF16)
    nr = -(-ns // (2 * SUBLANE)) * (2 * SUBLANE)
    imp = (_dot(ovt_ref[...], p_hi) + _dot(ovt_ref[...], p_lo))[0:nr]
    jrow = lax.broadcasted_iota(jnp.int32, (nr, tq), 0)
    jrow_f = jrow.astype(F32)
    bt = (i * tq + lax.broadcasted_iota(jnp.int32, (nr, tq), 1)) // SEL_L
    in_range = jrow < ns
    forced = (jrow == 0) | (jrow == bt) | (jrow == bt - 1)
    val = jnp.where(forced | (jrow > bt), -1.0, imp)
    val = jnp.where(in_range, val, -3.0)
    sel = jnp.where(forced, 1.0, 0.0)
    for _ in range(n_sel - 3):
        top = jnp.max(val, axis=0, keepdims=True)
        first = jnp.min(jnp.where(val == top, jrow_f, 1e3), axis=0, keepdims=True)
        hit = jrow_f == first
        sel = jnp.where(hit, 1.0, sel)
        val = jnp.where(hit, -5.0, val)
    bias = jnp.where(in_range & (sel == 0.0), SEL_BIAS, 0.0).astype(BF16)
    if nr < HEAD_DIM:
        bias = jnp.concatenate([bias, jnp.zeros((HEAD_DIM - nr, tq), BF16)], axis=0)
    q_sel = jnp.concatenate([qt, jnp.concatenate([bias] * HPG, axis=1)], axis=0)

    for back in range(1, n_back):
        win_score(back)
        if back == 1:
            win_edge_accumulate()
        else:
            win_accumulate(back - 1)
    if n_back == 1:
        win_edge_accumulate()
    else:
        win_accumulate(n_back - 1)

    buf_a, buf_b = (sa_s, mxa_s), (sb_s, mxb_s)

    def sel_score(kt, mask, buf):
        k0 = pl.multiple_of(kt * tq, tq)
        kaug = jnp.where(lane_k < HEAD_DIM, ks_ref[0, 0, pl.ds(k0, tq), :], en_ref[pl.ds(k0, tq), :])
        score(kaug, q_sel, buf, mask)

    def sel_accumulate(kt, buf):
        k0 = pl.multiple_of(kt * tq, tq)
        accumulate(buf, vst_ref[0, 0, :, pl.ds(k0, tq)], m_s, acc_s)

    sel_score(0, causal | (i > 0), buf_a)

    def sel_pair(j, carry):
        t = 2 * j
        sel_score(t + 1, None, buf_b)
        sel_accumulate(t, buf_a)
        sel_score(t + 2, None, buf_a)
        sel_accumulate(t + 1, buf_b)
        return carry

    def sel_quad(j, carry):
        sel_pair(2 * j, carry)
        sel_pair(2 * j + 1, carry)
        return carry

    n_pairs = jnp.maximum(i - 1, 0) // 2
    lax.fori_loop(0, n_pairs // 2, sel_quad, 0)

    @pl.when((n_pairs & 1) == 1)
    def _():
        sel_pair(n_pairs - 1, 0)

    @pl.when(i == 0)
    def _():
        sel_accumulate(0, buf_a)

    @pl.when((i & 1) == 1)
    def _():
        sel_score(i, causal, buf_b)
        sel_accumulate(i - 1, buf_a)
        sel_accumulate(i, buf_b)

    @pl.when(((i & 1) == 0) & (i > 0))
    def _():
        sel_score(i - 1, None, buf_b)
        sel_accumulate(i - 2, buf_a)
        sel_score(i, causal, buf_a)
        sel_accumulate(i - 1, buf_b)
        sel_accumulate(i, buf_a)

    gt = gt_ref[0, 0]
    o_cmp = ocmp_s[...]
    o_sel = normalized(acc_s)
    o_win = normalized(accw_s)
    for h in range(HPG):
        c0, c1 = h * tq, (h + 1) * tq
        o_h = (gt[h:h + 1] * o_cmp[:, c0:c1] + gt[SUBLANE + h:SUBLANE + h + 1] * o_sel[:, c0:c1]
               + gt[2 * SUBLANE + h:2 * SUBLANE + h + 1] * o_win[:, c0:c1])
        o_ref[0, h * HEAD_DIM:(h + 1) * HEAD_DIM, :] = o_h.astype(BF16)


def _attn_call(qt, kcn, vct, ksw, vt, gt, ovt, en, cm, tq):
    bsz, _, _, s = qt.shape
    ns = s // SEL_L
    cols = HPG * tq
    nct = kcn.shape[2]
    return pl.pallas_call(
        functools.partial(_attn_kernel, tq=tq, ns=ns, n_sel=min(N_SEL, ns)),
        grid=(bsz, N_KV, s // tq),
        in_specs=[pl.BlockSpec((1, HPG, HEAD_DIM, tq), lambda b, g, i: (b, g, 0, i)),
                  pl.BlockSpec((1, 1, nct, LANE), lambda b, g, i: (b, g, 0, 0)),
                  pl.BlockSpec((1, 1, HEAD_DIM, nct), lambda b, g, i: (b, g, 0, 0)),
                  pl.BlockSpec((1, 1, s, LANE), lambda b, g, i: (b, g, 0, 0)),
                  pl.BlockSpec((1, 1, s, LANE), lambda b, g, i: (b, N_KV + g, 0, 0)),
                  pl.BlockSpec((1, 1, V_ROWS, s), lambda b, g, i: (b, g, 0, 0)),
                  pl.BlockSpec((1, 1, V_ROWS, s), lambda b, g, i: (b, N_KV + g, 0, 0)),
                  pl.BlockSpec((1, 1, GATE_ROWS, tq), lambda b, g, i: (b, g, 0, i)),
                  pl.BlockSpec((HEAD_DIM, nct), lambda b, g, i: (0, 0)),
                  pl.BlockSpec((s, LANE), lambda b, g, i: (0, 0)),
                  pl.BlockSpec((tq, cols), lambda b, g, i: (0, 0))],
        out_specs=pl.BlockSpec((1, HPG * HEAD_DIM, tq), lambda b, g, i: (b, g, i)),
        out_shape=jax.ShapeDtypeStruct((bsz, ATTN_W, s), BF16),
        scratch_shapes=[pltpu.VMEM((1, cols), F32), pltpu.VMEM((V_ROWS, cols), F32),
                        pltpu.VMEM((1, cols), F32), pltpu.VMEM((V_ROWS, cols), F32),
                        pltpu.VMEM((HEAD_DIM, cols), F32),
                        pltpu.VMEM((tq, cols), F32), pltpu.VMEM((tq, cols), F32),
                        pltpu.VMEM((tq, cols), F32), pltpu.VMEM((tq, cols), F32),
                        pltpu.VMEM((1, cols), F32), pltpu.VMEM((1, cols), F32),
                        pltpu.VMEM((1, cols), F32), pltpu.VMEM((1, cols), F32)],
        compiler_params=_cparams(3),
        name="nsa_attn",
    )(qt, kcn, vct, ksw, ksw, vt, vt, gt, ovt, en, cm)


def _merge_kernel(x_ref, mod_ref, g_ref, uconv_ref, ot_ref, wmg_ref, bmg_ref, wuc_ref, wua_ref, wo_ref, out_ref):
    x = x_ref[0]
    mod = mod_ref[0]
    d = x.shape[-1]
    hb = _modulated_norm(x, g_ref[...], mod[0:1], mod[1:2]).astype(BF16)
    y_conv = _dot(uconv_ref[0], wuc_ref[...])
    y_attn = _dot_tn(ot_ref[0], wua_ref[...])
    g_conv = jax.nn.sigmoid(_dot(hb, wmg_ref[:, 0:d]) + bmg_ref[:, 0:d])
    g_attn = jax.nn.sigmoid(_dot(hb, wmg_ref[:, d:2 * d]) + bmg_ref[:, d:2 * d])
    mixed = _dot((g_conv * y_conv + g_attn * y_attn).astype(BF16), wo_ref[...])
    out_ref[0] = x + mod[2:3] * mixed


def _merge_call(x, mod, g, uconv, ot, l, w_mg, b_mg, w_uc, w_ua, w_o, tm):
    bsz, s, d = x.shape
    return pl.pallas_call(
        _merge_kernel,
        grid=(bsz, s // tm),
        in_specs=[pl.BlockSpec((1, tm, d), lambda b, i: (b, i, 0)),
                  pl.BlockSpec((1, 6, d), lambda b, i: (b, 0, 0)),
                  _whole(1, d),
                  pl.BlockSpec((1, tm, CONV_W), lambda b, i: (b, i, 0)),
                  pl.BlockSpec((1, ATTN_W, tm), lambda b, i: (b, 0, i)),
                  _layer(l, d, 2 * d), _layer(l, 1, 2 * d), _layer(l, CONV_W, d), _layer(l, ATTN_W, d),
                  _layer(l, d, d)],
        out_specs=pl.BlockSpec((1, tm, d), lambda b, i: (b, i, 0)),
        out_shape=jax.ShapeDtypeStruct((bsz, s, d), F32),
        compiler_params=_cparams(2),
        name="merge_out",
    )(x, mod, g, uconv, ot, w_mg, b_mg, w_uc, w_ua, w_o)


def _mlp_kernel(x_ref, mod_ref, g_ref, w1_ref, w2_ref, *rest, tf):
    out_ref = rest[-1]
    x = x_ref[0]
    mod = mod_ref[0]
    hb = _modulated_norm(x, g_ref[...], mod[3:4], mod[4:5]).astype(BF16)
    acc = None
    for c in range(w1_ref.shape[1] // tf):
        a = jnp.maximum(_dot(hb, w1_ref[:, c * tf:(c + 1) * tf]), 0.0)
        part = _dot((a * a).astype(BF16), w2_ref[c * tf:(c + 1) * tf, :])
        acc = part if acc is None else acc + part
    y = x + mod[5:6] * acc
    if len(rest) == 2:
        ms = jnp.mean(y * y, axis=-1, keepdims=True)
        y = y * lax.rsqrt(ms + EPS) * rest[0][...]
    out_ref[0] = y


def _mlp_call(x, mod, g, l, w1, w2, g_final, tm, tf):
    bsz, s, d = x.shape
    d_ff = w1.shape[-1]
    extra = () if g_final is None else (g_final,)
    return pl.pallas_call(
        functools.partial(_mlp_kernel, tf=tf),
        grid=(bsz, s // tm),
        in_specs=[pl.BlockSpec((1, tm, d), lambda b, i: (b, i, 0)),
                  pl.BlockSpec((1, 6, d), lambda b, i: (b, 0, 0)),
                  _whole(1, d), _layer(l, d, d_ff), _layer(l, d_ff, d)] + [_whole(1, d)] * len(extra),
        out_specs=pl.BlockSpec((1, tm, d), lambda b, i: (b, i, 0)),
        out_shape=jax.ShapeDtypeStruct((bsz, s, d), F32),
        compiler_params=_cparams(2),
        name="relu2_mlp",
    )(x, mod, g, w1, w2, *extra)


def _pack_weights(w_in, b_in, cmp_k_w2, cmp_v_w2):
    offs = np.concatenate([[0], np.cumsum(IN_SIZES)])
    col = lambda a, k: a[..., int(offs[k]):int(offs[k + 1])]

    def pad_groups(a):
        lead = a.shape[:-1]
        a = a.reshape(lead + (N_KV, HEAD_DIM))
        a = jnp.pad(a, [(0, 0)] * len(lead) + [(0, 0), (0, LANE - HEAD_DIM)])
        return a.reshape(lead + (N_KV * LANE,))

    def gate_slots(a):
        lead = a.shape[:-1]
        a = a.reshape(lead + (3, N_KV, HPG))
        a = jnp.moveaxis(a, -2, -3)
        a = jnp.pad(a, [(0, 0)] * (len(lead) + 2) + [(0, SUBLANE - HPG)])
        return a.reshape(lead + (N_KV * GATE_ROWS,))

    def main(a):
        return jnp.concatenate([col(a, 0), col(a, 1), col(a, 2), col(a, 4), col(a, 5),
                                pad_groups(col(a, 6)), pad_groups(col(a, 8))], axis=-1)

    def trans(a):
        return jnp.concatenate([col(a, 3), col(a, 7), col(a, 9), gate_slots(col(a, 10))], axis=-1)

    w_main = main(w_in).astype(BF16)
    b_main = main(b_in)[:, None, :]
    w_t = jnp.swapaxes(trans(w_in), 1, 2).astype(BF16)
    b_t = trans(b_in)[:, :, None]
    w_mg = col(w_in, 11).astype(BF16)
    b_mg = col(b_in, 11)[:, None, :]
    w2k = jnp.pad(cmp_k_w2, [(0, 0), (0, 0), (0, LANE - HEAD_DIM)]).astype(BF16)
    w2vt = jnp.swapaxes(cmp_v_w2, 1, 2).astype(BF16)
    return w_main, b_main, w_t, b_t, w_mg, b_mg, w2k, w2vt


def _rope_tables(positions):
    half = HEAD_DIM // 2
    inv_freq = ROPE_THETA ** (-jnp.arange(half, dtype=F32) * 2.0 / HEAD_DIM)
    pos = positions.astype(F32)
    ang = pos[..., None] * jnp.tile(inv_freq, LANE // half)
    cos, sin = jnp.cos(ang), jnp.sin(ang)
    first = (np.arange(LANE) % HEAD_DIM) < half
    lane_tab = jnp.stack([cos, jnp.where(first, -sin, 0.0), jnp.where(first, 0.0, sin)], axis=1)
    ang_t = inv_freq[None, :, None] * pos[:, None, :]
    t_tab = jnp.stack([jnp.cos(ang_t), jnp.sin(ang_t)], axis=1)
    return lane_tab, t_tab


def _static_tables(s, nchunk, tq):
    ns = s // SEL_L
    cstart = np.arange(nchunk) * CMP_D
    jj = np.arange(ns)
    overlap = (cstart[:, None] < (jj[None, :] + 1) * SEL_L) & (cstart[:, None] + CMP_L > jj[None, :] * SEL_L)
    ovt = np.zeros((HEAD_DIM, nchunk), np.float32)
    ovt[:ns] = overlap.T
    en = np.zeros((s, LANE), np.float32)
    en[np.arange(s), HEAD_DIM + np.arange(s) // SEL_L] = 1.0
    cm = np.tile(np.arange(tq)[:, None] <= np.arange(tq)[None, :], (1, HPG)).astype(np.float32)
    return jnp.asarray(ovt, BF16), jnp.asarray(en, BF16), jnp.asarray(cm, BF16)


def kernel(x, c, positions, w_ada, b_ada, g_mix, w_in, b_in, conv_w, cmp_k_w1, cmp_k_w2, cmp_k_pe, cmp_v_w1, cmp_v_w2, cmp_v_pe, w_up_conv, w_up_attn, w_o, g_mlp, w_ff1, w_ff2, g_final):
    bsz, s, d = x.shape
    depth = w_ada.shape[0]
    nchunk = s // CMP_D
    tm = 1024
    tq = 256
    nb = 4 if bsz % 4 == 0 else 1
    assert s % tm == 0 and s % tq == 0 and WINDOW % tq == 0 and nchunk == LANE and N_SEL <= s // SEL_L <= HEAD_DIM

    w_main, b_main, w_t, b_t, w_mg, b_mg, w2k, w2vt = _pack_weights(w_in, b_in, cmp_k_w2, cmp_v_w2)
    w1k, w1v = cmp_k_w1.astype(BF16), cmp_v_w1.astype(BF16)
    pek = jnp.broadcast_to(cmp_k_pe.reshape(depth, 1, CMP_L * HEAD_DIM), (depth, 8, CMP_L * HEAD_DIM))
    pev = jnp.broadcast_to(cmp_v_pe.reshape(depth, 1, CMP_L * HEAD_DIM), (depth, 8, CMP_L * HEAD_DIM))
    w_uc, w_ua, w_ob = w_up_conv.astype(BF16), w_up_attn.astype(BF16), w_o.astype(BF16)
    w_f1, w_f2 = w_ff1.astype(BF16), w_ff2.astype(BF16)
    rope_tab, rope_tab_t = _rope_tables(positions)
    ovt, en, cm = _static_tables(s, nchunk, tq)

    cond = _ada_call(c, w_ada, b_ada).reshape(depth, bsz, 6, d)

    for l in range(depth):
        mod = cond[l]
        uconv, ukv, ksw, qt, vt, gt = _in_call(x, mod, g_mix[l][None], l, w_main, b_main, w_t, b_t,
                                               rope_tab, rope_tab_t, conv_w, tm)
        kcn, vct = _cmp_call(ukv, l, w1k, w1v, pek, pev, w2k, w2vt, nb)
        ot = _attn_call(qt, kcn, vct, ksw, vt, gt, ovt, en, cm, tq)
        x = _merge_call(x, mod, g_mix[l][None], uconv, ot, l, w_mg, b_mg, w_uc, w_ua, w_ob, tm)
        x = _mlp_call(x, mod, g_mlp[l][None], l, w_f1, w_f2, g_final[None] if l == depth - 1 else None, tm, 1024)
    return x
```
